```python
import math
import jax, jax.numpy as jnp
from jax import lax
import numpy as np

D_MODEL = 1024
BATCH = 4
SEQ = 4096
DEPTH = 2

N_EVEN = (DEPTH + 1) // 2
N_ODD = DEPTH // 2

POOL_WINDOWS = (2, 4, 8, 16)
N_POOL_GROUPS = len(POOL_WINDOWS)
POOL_DIM = D_MODEL // 2
POOL_GROUP = POOL_DIM // N_POOL_GROUPS

HEAD_DIM = 64
N_HEADS = (D_MODEL // 2) // HEAD_DIM
N_KV_HEADS = 2
GQ = N_HEADS // N_KV_HEADS
Q_DIM = N_HEADS * HEAD_DIM
KV_DIM = N_KV_HEADS * HEAD_DIM
WINDOW = 128
BLOCK = 128
ROPE_THETA = 10000.0
MIX_IN_DIM = POOL_DIM + Q_DIM + 2 * KV_DIM
MIX_OUT_DIM = POOL_DIM + Q_DIM
MAX_POS_OFFSET = 1024

SSM_EXPAND = 2
SSM_D_INNER = SSM_EXPAND * D_MODEL
SSM_HEAD_DIM = 64
SSM_HEADS = SSM_D_INNER // SSM_HEAD_DIM
SSM_GROUPS = 8
SSM_STATE = 128
SSM_CONV = 4
SSM_CHUNK = 128
SSM_CONV_DIM = SSM_D_INNER + 2 * SSM_GROUPS * SSM_STATE
SSM_IN_DIM = SSM_D_INNER + SSM_CONV_DIM + SSM_HEADS

D_FF = 2816
FFN_CONV = 3

NORM_EPS = 1e-6
SSM_NORM_EPS = 1e-5

kernel_name = "hybrid_pool_swa_ssd_convffn"


def rms_norm(x, w, eps=NORM_EPS):
    xf = x.astype(jnp.float32)
    y = xf * lax.rsqrt(jnp.mean(xf * xf, axis=-1, keepdims=True) + eps)
    return (y * w.astype(jnp.float32)).astype(x.dtype)


def causal_dwconv(x, w, b):
    K, C = w.shape
    y = lax.conv_general_dilated(
        x, w.astype(x.dtype)[:, None, :], window_strides=(1,), padding=[(K - 1, 0)],
        dimension_numbers=("NWC", "WIO", "NWC"), feature_group_count=C)
    return y + b.astype(x.dtype)


def rope_tables(positions):
    inv_freq = ROPE_THETA ** (-jnp.arange(0, HEAD_DIM, 2, dtype=jnp.float32) / HEAD_DIM)
    ang = positions.astype(jnp.float32)[..., None] * inv_freq
    ang = jnp.concatenate([ang, ang], axis=-1)
    return jnp.cos(ang)[:, :, None, :], jnp.sin(ang)[:, :, None, :]


def apply_rope(t, cos, sin):
    tf = t.astype(jnp.float32)
    half = HEAD_DIM // 2
    rot = jnp.concatenate([-tf[..., half:], tf[..., :half]], axis=-1)
    return (tf * cos + rot * sin).astype(t.dtype)


def multiscale_pool(u):
    B, S, _ = u.shape
    ug = u.reshape(B, S, N_POOL_GROUPS, POOL_GROUP).astype(jnp.float32)
    cs = jnp.pad(jnp.cumsum(ug, axis=1), ((0, 0), (1, 0), (0, 0), (0, 0)))
    t1 = jnp.arange(1, S + 1)
    means = []
    for g, w in enumerate(POOL_WINDOWS):
        upper = cs[:, 1:, g]
        lower = jnp.pad(cs[:, :S + 1 - w, g], ((0, 0), (w - 1, 0), (0, 0)))
        cnt = jnp.minimum(t1, w).astype(jnp.float32)[None, :, None]
        means.append((upper - lower) / cnt)
    return jnp.stack(means, axis=2) - ug


def sliding_window_attention(q, k, v, sinks):
    B, S, _, _ = q.shape
    nb = S // BLOCK
    qb = q.reshape(B, nb, BLOCK, N_KV_HEADS, GQ, HEAD_DIM)

    def with_prev(t):
        t = t.reshape(B, nb, BLOCK, N_KV_HEADS, HEAD_DIM)
        prev = jnp.pad(t[:, :-1], ((0, 0), (1, 0), (0, 0), (0, 0), (0, 0)))
        return jnp.concatenate([prev, t], axis=2)

    kk, vv = with_prev(k), with_prev(v)
    s = jnp.einsum("bnqkgd,bnskd->bkgnqs", qb, kk).astype(jnp.float32) * (HEAD_DIM ** -0.5)
    qi = jnp.arange(BLOCK)[:, None]
    kj = jnp.arange(2 * BLOCK)[None, :]
    rel = qi + BLOCK - kj
    band = (rel >= 0) & (rel < WINDOW)
    valid = (jnp.arange(nb)[:, None, None] > 0) | (kj[None] >= BLOCK)
    mask = band[None] & valid
    s = jnp.where(mask, s, -jnp.inf)
    sink = jnp.broadcast_to(sinks.astype(jnp.float32).reshape(1, N_KV_HEADS, GQ, 1, 1, 1),
                            s.shape[:-1] + (1,))
    p = jax.nn.softmax(jnp.concatenate([s, sink], axis=-1), axis=-1)[..., :-1]
    o = jnp.einsum("bkgnqs,bnskd->bnqkgd", p.astype(v.dtype), vv)
    return o.reshape(B, S, Q_DIM)


def pool_attention_mixer(h, cos, sin, w_in, pool_w, pool_scale, sinks, w_out):
    B, S, _ = h.shape
    proj = h @ w_in
    u, q, k, v = jnp.split(proj, [POOL_DIM, POOL_DIM + Q_DIM, POOL_DIM + Q_DIM + KV_DIM], axis=-1)
    pooled = multiscale_pool(u).astype(h.dtype)
    pooled = jnp.einsum("bsgc,gcd->bsgd", pooled, pool_w).reshape(B, S, POOL_DIM) * pool_scale
    q = apply_rope(q.reshape(B, S, N_HEADS, HEAD_DIM), cos, sin)
    k = apply_rope(k.reshape(B, S, N_KV_HEADS, HEAD_DIM), cos, sin)
    v = v.reshape(B, S, N_KV_HEADS, HEAD_DIM)
    attn = sliding_window_attention(q, k, v, sinks)
    return jnp.concatenate([pooled, attn], axis=-1) @ w_out


def ssd_scan(x, dt, A, Bm, Cm):
    b, s, h, p = x.shape
    g, n = Bm.shape[2:]
    r = h // g
    c = s // SSM_CHUNK
    X = (x * dt[..., None]).reshape(b, c, SSM_CHUNK, g, r, p)
    a = (dt * A).reshape(b, c, SSM_CHUNK, g, r).transpose(0, 3, 4, 1, 2)
    a_cs = jnp.cumsum(a, axis=-1)
    Bc = Bm.reshape(b, c, SSM_CHUNK, g, n)
    Cc = Cm.reshape(b, c, SSM_CHUNK, g, n)
    tril = jnp.tril(jnp.ones((SSM_CHUNK, SSM_CHUNK), dtype=bool))
    seg = a_cs[..., :, None] - a_cs[..., None, :]
    Lmat = jnp.exp(jnp.where(tril, seg, -jnp.inf))
    CB = jnp.einsum("bclgn,bcsgn->bgcls", Cc, Bc)
    y_diag = jnp.einsum("bgcls,bgrcls,bcsgrp->bclgrp", CB, Lmat, X)
    decay = jnp.exp(a_cs[..., -1:] - a_cs)
    states = jnp.einsum("bclgn,bgrcl,bclgrp->bcgrpn", Bc, decay, X)
    chunk_decay = jnp.exp(a_cs[..., -1])

    def step(state, inp):
        st, dec = inp
        return state * dec[..., None, None] + st, state

    h0 = jnp.zeros((b, g, r, p, n), jnp.float32)
    _, prev = lax.scan(step, h0, (jnp.moveaxis(states, 1, 0), jnp.moveaxis(chunk_decay, 3, 0)))
    y_off = jnp.einsum("bclgn,cbgrpn,bgrcl->bclgrp", Cc, prev, jnp.exp(a_cs))
    return (y_diag + y_off).reshape(b, s, h, p)


def ssd_mixer(h, w_in, conv_w, conv_b, dt_bias, A_log, D_skip, norm_w, w_out):
    B, S, _ = h.shape
    proj = h @ w_in
    z, xbc, dt = jnp.split(proj, [SSM_D_INNER, SSM_D_INNER + SSM_CONV_DIM], axis=-1)
    xbc = jax.nn.silu(causal_dwconv(xbc, conv_w, conv_b))
    xs, Bm, Cm = jnp.split(xbc, [SSM_D_INNER, SSM_D_INNER + SSM_GROUPS * SSM_STATE], axis=-1)
    xs = xs.reshape(B, S, SSM_HEADS, SSM_HEAD_DIM).astype(jnp.float32)
    Bm = Bm.reshape(B, S, SSM_GROUPS, SSM_STATE).astype(jnp.float32)
    Cm = Cm.reshape(B, S, SSM_GROUPS, SSM_STATE).astype(jnp.float32)
    dt = jax.nn.softplus(dt.astype(jnp.float32) + dt_bias.astype(jnp.float32))
    A = -jnp.exp(A_log.astype(jnp.float32))
    y = ssd_scan(xs, dt, A, Bm, Cm) + D_skip.astype(jnp.float32)[:, None] * xs
    y = y.reshape(B, S, SSM_D_INNER) * jax.nn.silu(z.astype(jnp.float32))
    y = rms_norm(y, norm_w, SSM_NORM_EPS).astype(h.dtype)
    return y @ w_out


def conv_ffn(h, w_up, conv_w, conv_b, w_down):
    hid = causal_dwconv(h @ w_up, conv_w, conv_b)
    u, g = jnp.split(hid, 2, axis=-1)
    return (jax.nn.silu(g) * u) @ w_down


def setup_inputs(seed: int = 0) -> dict:
    key = jax.random.key(seed)
    ks = jax.random.split(key, 24)
    f32 = jnp.float32
    nrm = lambda k, shape, scale: jax.random.normal(k, shape, f32) * scale
    x = jax.random.normal(ks[0], (BATCH, SEQ, D_MODEL), f32)
    offsets = jax.random.randint(ks[1], (BATCH, 1), 0, MAX_POS_OFFSET, dtype=jnp.int32)
    positions = (offsets + jnp.arange(SEQ, dtype=jnp.int32)[None, :]).astype(jnp.int32)
    dt0 = jnp.exp(jax.random.uniform(ks[13], (N_ODD, SSM_HEADS), f32, math.log(1e-3), math.log(1e-1)))
    return {
        "x": x,
        "positions": positions,
        "norm_mix": 1.0 + nrm(ks[2], (DEPTH, D_MODEL), 0.02),
        "norm_ffn": 1.0 + nrm(ks[3], (DEPTH, D_MODEL), 0.02),
        "norm_final": 1.0 + nrm(ks[4], (D_MODEL,), 0.02),
        "mix_w_in": nrm(ks[5], (N_EVEN, D_MODEL, MIX_IN_DIM), D_MODEL ** -0.5),
        "pool_w": nrm(ks[6], (N_EVEN, N_POOL_GROUPS, POOL_GROUP, POOL_GROUP), POOL_GROUP ** -0.5),
        "pool_scale": 1.0 + nrm(ks[7], (N_EVEN, POOL_DIM), 0.1),
        "attn_sinks": nrm(ks[8], (N_EVEN, N_HEADS), 1.0),
        "mix_w_out": nrm(ks[9], (N_EVEN, MIX_OUT_DIM, D_MODEL), MIX_OUT_DIM ** -0.5),
        "ssm_w_in": nrm(ks[10], (N_ODD, D_MODEL, SSM_IN_DIM), D_MODEL ** -0.5),
        "ssm_conv_w": nrm(ks[11], (N_ODD, SSM_CONV, SSM_CONV_DIM), SSM_CONV ** -0.5),
        "ssm_conv_b": nrm(ks[12], (N_ODD, SSM_CONV_DIM), 0.02),
        "ssm_dt_bias": dt0 + jnp.log(-jnp.expm1(-dt0)),
        "ssm_A_log": jnp.log(jax.random.uniform(ks[14], (N_ODD, SSM_HEADS), f32, 1.0, 16.0)),
        "ssm_D": 1.0 + nrm(ks[15], (N_ODD, SSM_HEADS), 0.1),
        "ssm_norm": 1.0 + nrm(ks[16], (N_ODD, SSM_D_INNER), 0.02),
        "ssm_w_out": nrm(ks[17], (N_ODD, SSM_D_INNER, D_MODEL), SSM_D_INNER ** -0.5),
        "ffn_w_up": nrm(ks[18], (DEPTH, D_MODEL, 2 * D_FF), D_MODEL ** -0.5),
        "ffn_conv_w": nrm(ks[19], (DEPTH, FFN_CONV, 2 * D_FF), FFN_CONV ** -0.5),
        "ffn_conv_b": nrm(ks[20], (DEPTH, 2 * D_FF), 0.02),
        "ffn_w_down": nrm(ks[21], (DEPTH, D_FF, D_MODEL), D_FF ** -0.5),
    }


def reference(x, positions, norm_mix, norm_ffn, norm_final, mix_w_in, pool_w, pool_scale,
              attn_sinks, mix_w_out, ssm_w_in, ssm_conv_w, ssm_conv_b, ssm_dt_bias, ssm_A_log,
              ssm_D, ssm_norm, ssm_w_out, ffn_w_up, ffn_conv_w, ffn_conv_b, ffn_w_down):
    cos, sin = rope_tables(positions)
    for i in range(DEPTH):
        j = i // 2
        h = rms_norm(x, norm_mix[i])
        if i % 2 == 0:
            x = x + pool_attention_mixer(h, cos, sin, mix_w_in[j], pool_w[j], pool_scale[j],
                                         attn_sinks[j], mix_w_out[j])
        else:
            x = x + ssd_mixer(h, ssm_w_in[j], ssm_conv_w[j], ssm_conv_b[j], ssm_dt_bias[j],
                              ssm_A_log[j], ssm_D[j], ssm_norm[j], ssm_w_out[j])
        x = x + conv_ffn(rms_norm(x, norm_ffn[i]), ffn_w_up[i], ffn_conv_w[i], ffn_conv_b[i],
                         ffn_w_down[i])
    return rms_norm(x, norm_final)
```

```python
import functools
import math

import numpy as np
import jax
import jax.numpy as jnp
from jax import lax
from jax.experimental import pallas as pl
from jax.experimental.pallas import tpu as pltpu

HEAD_DIM = 64
N_HEADS = 8
N_KV_HEADS = 2
GQ = N_HEADS // N_KV_HEADS
ATT_BLOCK = 128
ROPE_THETA = 10000.0
POOL_WINDOWS = (2, 4, 8, 16)
SSM_HEAD_DIM = 64
SSM_HEADS = 32
SSM_GROUPS = 8
SSM_STATE = 128
SSM_CONV = 4
SSM_CHUNK = 128
FFN_CONV = 3
NORM_EPS = 1e-6
SSM_NORM_EPS = 1e-5

LANES = 128
SUBLANES = 8
VMEM_LIMIT_BYTES = 56 * 1024 * 1024

F32 = jnp.float32
BF16 = jnp.bfloat16


def _rms(x, w, eps):
    ms = jnp.mean(x * x, axis=-1, keepdims=True)
    return x * lax.rsqrt(ms + eps) * w


def _silu(x):
    return x * jax.nn.sigmoid(x)


def _const_spec(shape):
    nd = len(shape)
    return pl.BlockSpec(shape, lambda i: (0,) * nd, pipeline_mode=pl.Buffered(1))


def _split3_f32(x):
    hi = x.astype(BF16).astype(F32)
    r1 = x - hi
    mid = r1.astype(BF16).astype(F32)
    lo = (r1 - mid).astype(BF16).astype(F32)
    return hi, mid, lo


def _mix0_body(x_ref, pos_ref, nw_ref, win_ref, invf_ref, sgn_ref, poolw_ref, pscale_ref,
               sink_ref, wout_ref, o_ref, ubuf, kbuf, vbuf, qbuf, mixbuf, *, tm, tiles_per_seq):
    i = pl.program_id(0)
    tile_in_seq = lax.rem(i, tiles_per_seq)
    nblk = tm // ATT_BLOCK
    halo = max(POOL_WINDOWS)
    pool_dim = LANES * len(POOL_WINDOWS)
    q_dim = N_HEADS * HEAD_DIM
    nslab = q_dim // LANES

    @pl.when(tile_in_seq == 0)
    def _():
        ubuf[:, 0:halo, :] = jnp.zeros((len(POOL_WINDOWS), halo, LANES), F32)
        kbuf[0:ATT_BLOCK, :] = jnp.zeros((ATT_BLOCK, LANES), BF16)
        vbuf[0:ATT_BLOCK, :] = jnp.zeros((ATT_BLOCK, LANES), BF16)

    x = x_ref[...]
    hb = _rms(x, nw_ref[...], NORM_EPS).astype(BF16)
    proj = jnp.dot(hb, win_ref[...], preferred_element_type=F32)

    ang = pos_ref[...].astype(F32) * invf_ref[...]
    cs = jnp.cos(ang)
    sn = jnp.sin(ang) * sgn_ref[...]
    lane = lax.broadcasted_iota(jnp.int32, (tm, LANES), 1)
    first_half = (lane & (HEAD_DIM // 2)) == 0
    low_head = lane < HEAD_DIM

    def rope(t):
        up = pltpu.roll(t, LANES - HEAD_DIM // 2, 1)
        dn = pltpu.roll(t, HEAD_DIM // 2, 1)
        return t * cs + jnp.where(first_half, up, dn) * sn

    scale = HEAD_DIM ** -0.5
    for j in range(nslab):
        q = rope(proj[:, pool_dim + LANES * j: pool_dim + LANES * (j + 1)]) * scale
        qa = jnp.where(low_head, q, 0.0).astype(BF16)
        qb = jnp.where(low_head, 0.0, q).astype(BF16)
        for b in range(nblk):
            rows = slice(ATT_BLOCK * b, ATT_BLOCK * (b + 1))
            qbuf[j, b, 0:ATT_BLOCK, :] = qa[rows]
            qbuf[j, b, ATT_BLOCK:2 * ATT_BLOCK, :] = qb[rows]
    kcol = pool_dim + q_dim
    kbuf[ATT_BLOCK:ATT_BLOCK + tm, :] = rope(proj[:, kcol:kcol + LANES]).astype(BF16)
    vbuf[ATT_BLOCK:ATT_BLOCK + tm, :] = proj[:, kcol + LANES:kcol + 2 * LANES].astype(BF16)

    tpos = tile_in_seq * tm + lax.broadcasted_iota(jnp.int32, (tm, 1), 0)
    for g, w in enumerate(POOL_WINDOWS):
        u_g = proj[:, LANES * g:LANES * (g + 1)]
        ubuf[g, halo:halo + tm, :] = u_g
        acc = u_g
        for k in range(1, w):
            acc = acc + ubuf[g, halo - k:halo - k + tm, :]
        cnt = jnp.minimum(tpos + 1, w).astype(F32)
        pooled = acc / cnt - u_g
        pm = jnp.dot(pooled.astype(BF16), poolw_ref[g], preferred_element_type=F32)
        mixbuf[:, LANES * g:LANES * (g + 1)] = (pm * pscale_ref[:, LANES * g:LANES * (g + 1)]).astype(BF16)
        ubuf[g, 0:halo, :] = ubuf[g, tm:tm + halo, :]

    qi = lax.broadcasted_iota(jnp.int32, (2 * ATT_BLOCK, 2 * ATT_BLOCK), 0) & (ATT_BLOCK - 1)
    kj = lax.broadcasted_iota(jnp.int32, (2 * ATT_BLOCK, 2 * ATT_BLOCK), 1)
    rel = qi + ATT_BLOCK - kj
    band = (rel >= 0) & (rel < ATT_BLOCK)
    first_lim = jnp.where(tile_in_seq == 0, ATT_BLOCK, 0)
    row2 = lax.broadcasted_iota(jnp.int32, (2 * ATT_BLOCK, 1), 0)
    lane_o = lax.broadcasted_iota(jnp.int32, (ATT_BLOCK, LANES), 1)
    for b in range(nblk):
        kk = kbuf[ATT_BLOCK * b:ATT_BLOCK * (b + 2), :]
        vv = vbuf[ATT_BLOCK * b:ATT_BLOCK * (b + 2), :]
        mask = (band & (kj >= first_lim)) if b == 0 else band
        for j in range(nslab):
            s = lax.dot_general(qbuf[j, b], kk, (((1,), (1,)), ((), ())),
                                preferred_element_type=F32)
            s = jnp.where(mask, s, -jnp.inf)
            sink = jnp.where(row2 < ATT_BLOCK, sink_ref[j], sink_ref[j + GQ])
            m = jnp.maximum(jnp.max(s, axis=-1, keepdims=True), sink)
            e = jnp.exp(s - m)
            den = jnp.sum(e, axis=-1, keepdims=True) + jnp.exp(sink - m)
            pv = jnp.dot(e.astype(BF16), vv, preferred_element_type=F32) / den
            o = jnp.where(lane_o < HEAD_DIM, pv[0:ATT_BLOCK], pv[ATT_BLOCK:2 * ATT_BLOCK])
            mixbuf[ATT_BLOCK * b:ATT_BLOCK * (b + 1),
                   pool_dim + LANES * j:pool_dim + LANES * (j + 1)] = o.astype(BF16)
    kbuf[0:ATT_BLOCK, :] = kbuf[tm:tm + ATT_BLOCK, :]
    vbuf[0:ATT_BLOCK, :] = vbuf[tm:tm + ATT_BLOCK, :]

    o_ref[...] = x + jnp.dot(mixbuf[...], wout_ref[...], preferred_element_type=F32)


def _mix0_layer(xt, pos, nw, w_in, pool_w, pool_scale, sinks, w_out, *, seq, tm):
    t, d = xt.shape
    pool_dim = LANES * len(POOL_WINDOWS)
    q_dim = N_HEADS * HEAD_DIM
    nslab = q_dim // LANES
    head_order = [h for j in range(GQ) for h in (j, j + GQ)]
    qperm = np.concatenate([np.arange(HEAD_DIM) + HEAD_DIM * h for h in head_order])
    w_in_p = jnp.concatenate(
        [w_in[:, :pool_dim], w_in[:, pool_dim:pool_dim + q_dim][:, qperm], w_in[:, pool_dim + q_dim:]],
        axis=1).astype(BF16)
    w_out_p = jnp.concatenate([w_out[:pool_dim], w_out[pool_dim:][qperm]], axis=0).astype(BF16)
    inv_freq = ROPE_THETA ** (-np.arange(0, HEAD_DIM, 2, dtype=np.float32) / HEAD_DIM)
    invf = jnp.asarray(np.tile(inv_freq, LANES // (HEAD_DIM // 2))[None, :], F32)
    sgn = jnp.asarray(np.where((np.arange(LANES) % HEAD_DIM) < HEAD_DIM // 2, -1.0, 1.0)[None, :], F32)
    mix_in = w_in_p.shape[1]
    nblk = tm // ATT_BLOCK
    halo = max(POOL_WINDOWS)
    body = functools.partial(_mix0_body, tm=tm, tiles_per_seq=seq // tm)
    return pl.pallas_call(
        body,
        out_shape=jax.ShapeDtypeStruct((t, d), F32),
        grid=(t // tm,),
        in_specs=[
            pl.BlockSpec((tm, d), lambda i: (i, 0)),
            pl.BlockSpec((tm, 1), lambda i: (i, 0)),
            _const_spec((1, d)),
            _const_spec((d, mix_in)),
            _const_spec((1, LANES)),
            _const_spec((1, LANES)),
            _const_spec(pool_w.shape),
            _const_spec((1, pool_dim)),
            pl.BlockSpec(memory_space=pltpu.SMEM),
            _const_spec((pool_dim + q_dim, d)),
        ],
        out_specs=pl.BlockSpec((tm, d), lambda i: (i, 0)),
        scratch_shapes=[
            pltpu.VMEM((len(POOL_WINDOWS), halo + tm, LANES), F32),
            pltpu.VMEM((ATT_BLOCK + tm, LANES), BF16),
            pltpu.VMEM((ATT_BLOCK + tm, LANES), BF16),
            pltpu.VMEM((nslab, nblk, 2 * ATT_BLOCK, LANES), BF16),
            pltpu.VMEM((tm, pool_dim + q_dim), BF16),
        ],
        compiler_params=pltpu.CompilerParams(dimension_semantics=("arbitrary",),
                                             vmem_limit_bytes=VMEM_LIMIT_BYTES),
        name="pool_swa_mixer",
    )(xt, pos, nw.reshape(1, d), w_in_p, invf, sgn, pool_w.astype(BF16),
      pool_scale.reshape(1, pool_dim), sinks, w_out_p)


def _ssd_body(x_ref, nw_ref, wz_ref, wxbc_ref, wdt_ref, cw_ref, cb_ref, dtb_ref, alog_ref,
              dvec_ref, gnw_ref, wout_ref, tril_ref, e3_ref, eh_ref, o_ref,
              hb_ref, s_ref, halo_ref, xs_ref, xsb_ref, bt_ref, c_ref, y_ref, state_ref,
              acol_ref, acst_ref, dtt_ref, wb_ref, eab_ref, *, tm, tiles_per_seq):
    i = pl.program_id(0)
    tile_in_seq = lax.rem(i, tiles_per_seq)
    d_inner = SSM_HEADS * SSM_HEAD_DIM
    bc_dim = SSM_GROUPS * SSM_STATE
    n_xs = d_inner // LANES
    n_b = bc_dim // LANES
    n_conv_slabs = n_xs + 2 * n_b
    cw_cols = 4 * LANES
    slabs_per_chunk = cw_cols // LANES
    nchunk = tm // SSM_CHUNK
    L = SSM_CHUNK
    heads_per_group = SSM_HEADS // SSM_GROUPS
    gw = heads_per_group * SSM_HEAD_DIM

    @pl.when(tile_in_seq == 0)
    def _():
        halo_ref[...] = jnp.zeros(halo_ref.shape, halo_ref.dtype)
        state_ref[...] = jnp.zeros(state_ref.shape, state_ref.dtype)

    x = x_ref[...]
    hb_ref[...] = _rms(x, nw_ref[...], NORM_EPS).astype(BF16)

    for cc in range(n_conv_slabs // slabs_per_chunk):
        hid = jnp.dot(hb_ref[...], wxbc_ref[:, cw_cols * cc:cw_cols * (cc + 1)],
                      preferred_element_type=F32)
        for jj in range(slabs_per_chunk):
            j = slabs_per_chunk * cc + jj
            cols = slice(LANES * j, LANES * (j + 1))
            x0 = hid[:, LANES * jj:LANES * (jj + 1)]
            s_ref[jj, 0:SUBLANES, :] = halo_ref[j]
            s_ref[jj, SUBLANES:SUBLANES + tm, :] = x0
            halo_ref[j] = x0[tm - SUBLANES:tm]
            y = x0 * cw_ref[SSM_CONV - 1:SSM_CONV, cols] + cb_ref[:, cols]
            for k in range(1, SSM_CONV):
                y = y + s_ref[jj, SUBLANES - k:SUBLANES - k + tm, :] * cw_ref[SSM_CONV - 1 - k:SSM_CONV - k, cols]
            y = _silu(y)
            if j < n_xs:
                xs_ref[j] = y
                xsb_ref[j] = y.astype(BF16)
            elif j < n_xs + n_b:
                for c in range(nchunk):
                    bt_ref[j - n_xs, c] = y[L * c:L * (c + 1)].T.astype(BF16)
            else:
                c_ref[j - n_xs - n_b] = y.astype(BF16)

    dt = jax.nn.softplus(jnp.dot(hb_ref[...], wdt_ref[...], preferred_element_type=F32) + dtb_ref[...])
    a = dt * (-jnp.exp(alog_ref[...]))
    lane = lax.broadcasted_iota(jnp.int32, (L, LANES), 1)

    def pack3(v):
        hi, mid, lo = _split3_f32(v)
        return jnp.where(lane < SSM_HEADS, hi, jnp.where(lane < 2 * SSM_HEADS, mid, lo)).astype(BF16)

    tril = tril_ref[...]
    li = lax.broadcasted_iota(jnp.int32, (L, L), 0)
    si = lax.broadcasted_iota(jnp.int32, (L, L), 1)
    causal = si <= li
    lane_x = lax.broadcasted_iota(jnp.int32, (L, LANES), 1)
    low_head = lane_x < SSM_HEAD_DIM

    for c in range(nchunk):
        rows = slice(L * c, L * (c + 1))
        a_c = a[rows]
        dt_c = dt[rows]
        hi, mid, lo = _split3_f32(a_c)
        a_cs = (jnp.dot(tril, hi.astype(BF16), preferred_element_type=F32)
                + jnp.dot(tril, mid.astype(BF16), preferred_element_type=F32)
                + jnp.dot(tril, lo.astype(BF16), preferred_element_type=F32))
        acst_ref[...] = a_cs.T
        dtt_ref[...] = dt_c.T
        a_last = a_cs[L - 1:L, :]
        wgt = dt_c * jnp.exp(a_last - a_cs)
        ea = jnp.exp(a_cs)
        acol = jnp.dot(pack3(a_cs), eh_ref[...], preferred_element_type=F32)
        for h in range(SSM_HEADS):
            acol_ref[h] = acol[:, LANES * h:LANES * (h + 1)]
        wb = jnp.dot(pack3(wgt), e3_ref[...], preferred_element_type=F32)
        eab = jnp.dot(pack3(ea), e3_ref[...], preferred_element_type=F32)
        for j in range(n_xs):
            wb_ref[j] = (xs_ref[j, rows, :] * wb[:, LANES * j:LANES * (j + 1)]).astype(BF16)
            eab_ref[j] = eab[:, LANES * j:LANES * (j + 1)]

        def group(g, carry):
            cc_ = c_ref[g, rows, :]
            bt = bt_ref[g, c]
            cb = jnp.dot(cc_, bt, preferred_element_type=F32)
            prev = state_ref[g]
            yoff = jnp.dot(cc_, prev.astype(BF16), preferred_element_type=F32)
            xd = jnp.concatenate([wb_ref[2 * g], wb_ref[2 * g + 1]], axis=1)
            st_new = jnp.dot(bt, xd, preferred_element_type=F32)
            cd = jnp.concatenate([eab_ref[2 * g, L - 1:L, :], eab_ref[2 * g + 1, L - 1:L, :]], axis=1)
            state_ref[g] = prev * cd + st_new
            for pair in range(heads_per_group // 2):
                slab = 2 * g + pair
                ms = []
                for r in range(2):
                    h = heads_per_group * g + 2 * pair + r
                    seg = acol_ref[h] - acst_ref[pl.ds(h, 1), :]
                    lm = jnp.exp(jnp.where(causal, seg, -jnp.inf))
                    ms.append((cb * lm * dtt_ref[pl.ds(h, 1), :]).astype(BF16))
                mcat = jnp.concatenate(ms, axis=1)
                xpair = xsb_ref[slab, rows, :]
                zero = jnp.zeros_like(xpair)
                rhs = jnp.concatenate([jnp.where(low_head, xpair, zero),
                                       jnp.where(low_head, zero, xpair)], axis=0)
                yd = jnp.dot(mcat, rhs, preferred_element_type=F32)
                y_ref[slab, rows, :] = (yd + yoff[:, LANES * pair:LANES * (pair + 1)] * eab_ref[slab]
                                        + dvec_ref[pl.ds(slab, 1), :] * xs_ref[slab, rows, :])
            return carry

        lax.fori_loop(0, SSM_GROUPS, group, 0)

    ssq = jnp.zeros((tm, 1), F32)
    zc = 4 * LANES
    for cc in range(d_inner // zc):
        z = jnp.dot(hb_ref[...], wz_ref[:, zc * cc:zc * (cc + 1)], preferred_element_type=F32)
        for jj in range(zc // LANES):
            j = (zc // LANES) * cc + jj
            yg = y_ref[j] * _silu(z[:, LANES * jj:LANES * (jj + 1)])
            y_ref[j] = yg
            ssq = ssq + jnp.sum(yg * yg, axis=-1, keepdims=True)
    rinv = lax.rsqrt(ssq * (1.0 / d_inner) + SSM_NORM_EPS)
    yn = jnp.concatenate(
        [(y_ref[j] * rinv * gnw_ref[:, LANES * j:LANES * (j + 1)]).astype(BF16) for j in range(n_xs)],
        axis=1)
    o_ref[...] = x + jnp.dot(yn, wout_ref[...], preferred_element_type=F32)


def _ssd_layer(xt, nw, w_in, conv_w, conv_b, dt_bias, a_log, d_skip, norm_w, w_out, *, seq, tm):
    t, d = xt.shape
    d_inner = SSM_HEADS * SSM_HEAD_DIM
    bc_dim = SSM_GROUPS * SSM_STATE
    conv_dim = d_inner + 2 * bc_dim
    n_xs = d_inner // LANES
    n_b = bc_dim // LANES
    nchunk = tm // SSM_CHUNK
    L = SSM_CHUNK
    copies = 3
    pad = LANES - copies * SSM_HEADS

    def lanes3(v):
        return jnp.concatenate([v] * copies + [jnp.zeros(v.shape[:-1] + (pad,), v.dtype)], axis=-1)

    wz = w_in[:, :d_inner].astype(BF16)
    wxbc = w_in[:, d_inner:d_inner + conv_dim].astype(BF16)
    wdt = lanes3(w_in[:, d_inner + conv_dim:]).astype(BF16)
    dtb = lanes3(dt_bias.reshape(1, SSM_HEADS))
    alog = lanes3(a_log.reshape(1, SSM_HEADS))
    dvec = jnp.repeat(d_skip, SSM_HEAD_DIM).reshape(n_xs, LANES)
    tril = jnp.asarray(np.tril(np.ones((L, L), np.float32)), BF16)
    rows_h = np.arange(LANES) % SSM_HEADS
    valid = (np.arange(LANES) < copies * SSM_HEADS)[:, None]
    e3 = jnp.asarray(((rows_h[:, None] == (np.arange(d_inner) // SSM_HEAD_DIM)[None, :]) & valid)
                     .astype(np.float32), BF16)
    eh = jnp.asarray(((rows_h[:, None] == (np.arange(SSM_HEADS * LANES) // LANES)[None, :]) & valid)
                     .astype(np.float32), BF16)
    body = functools.partial(_ssd_body, tm=tm, tiles_per_seq=seq // tm)
    return pl.pallas_call(
        body,
        out_shape=jax.ShapeDtypeStruct((t, d), F32),
        grid=(t // tm,),
        in_specs=[
            pl.BlockSpec((tm, d), lambda i: (i, 0)),
            _const_spec((1, d)),
            _const_spec((d, d_inner)),
            _const_spec((d, conv_dim)),
            _const_spec((d, LANES)),
            _const_spec((SSM_CONV, conv_dim)),
            _const_spec((1, conv_dim)),
            _const_spec((1, LANES)),
            _const_spec((1, LANES)),
            _const_spec((n_xs, LANES)),
            _const_spec((1, d_inner)),
            _const_spec((d_inner, d)),
            _const_spec((L, L)),
            _const_spec((LANES, d_inner)),
            _const_spec((LANES, SSM_HEADS * LANES)),
        ],
        out_specs=pl.BlockSpec((tm, d), lambda i: (i, 0)),
        scratch_shapes=[
            pltpu.VMEM((tm, d), BF16),
            pltpu.VMEM((4, SUBLANES + tm, LANES), F32),
            pltpu.VMEM((n_xs + 2 * n_b, SUBLANES, LANES), F32),
            pltpu.VMEM((n_xs, tm, LANES), F32),
            pltpu.VMEM((n_xs, tm, LANES), BF16),
            pltpu.VMEM((n_b, nchunk, SSM_STATE, L), BF16),
            pltpu.VMEM((n_b, tm, LANES), BF16),
            pltpu.VMEM((n_xs, tm, LANES), F32),
            pltpu.VMEM((SSM_GROUPS, SSM_STATE, 2 * LANES), F32),
            pltpu.VMEM((SSM_HEADS, L, LANES), F32),
            pltpu.VMEM((LANES, L), F32),
            pltpu.VMEM((LANES, L), F32),
            pltpu.VMEM((n_xs, L, LANES), BF16),
            pltpu.VMEM((n_xs, L, LANES), F32),
        ],
        compiler_params=pltpu.CompilerParams(dimension_semantics=("arbitrary",),
                                             vmem_limit_bytes=VMEM_LIMIT_BYTES),
        name="ssd_mixer",
    )(xt, nw.reshape(1, d), wz, wxbc, wdt, conv_w, conv_b.reshape(1, conv_dim), dtb, alog, dvec,
      norm_w.reshape(1, d_inner), w_out.astype(BF16), tril, e3, eh)


def _ffn_body(x_ref, nw_ref, wup_ref, cw_ref, cb_ref, wdn_ref, fnw_ref, o_ref,
              hb_ref, s_ref, halo_ref, *, tm, fc, n_chunks, tiles_per_seq, final_norm):
    i = pl.program_id(0)
    nslab = 2 * fc // LANES

    @pl.when(lax.rem(i, tiles_per_seq) == 0)
    def _():
        halo_ref[...] = jnp.zeros(halo_ref.shape, halo_ref.dtype)

    x = x_ref[...]
    hb_ref[...] = _rms(x, nw_ref[...], NORM_EPS).astype(BF16)
    o_ref[...] = x

    def chunk(c, carry):
        hid = jnp.dot(hb_ref[...], wup_ref[c], preferred_element_type=F32)
        cw = cw_ref[c]
        cb = cb_ref[c]
        ys = []
        for j in range(nslab):
            cols = slice(LANES * j, LANES * (j + 1))
            x0 = hid[:, cols]
            s_ref[j, 0:SUBLANES, :] = halo_ref[c, j]
            s_ref[j, SUBLANES:SUBLANES + tm, :] = x0
            halo_ref[c, j] = x0[tm - SUBLANES:tm]
            y = x0 * cw[FFN_CONV - 1:FFN_CONV, cols] + cb[:, cols]
            for k in range(1, FFN_CONV):
                y = y + s_ref[j, SUBLANES - k:SUBLANES - k + tm, :] * cw[FFN_CONV - 1 - k:FFN_CONV - k, cols]
            ys.append(y)
        u = jnp.concatenate(ys[:nslab // 2], axis=1)
        g = jnp.concatenate(ys[nslab // 2:], axis=1)
        act = (_silu(g) * u).astype(BF16)
        o_ref[...] += jnp.dot(act, wdn_ref[c], preferred_element_type=F32)
        return carry

    lax.fori_loop(0, n_chunks, chunk, 0)
    if final_norm:
        o_ref[...] = _rms(o_ref[...], fnw_ref[...], NORM_EPS)


def _ffn_layer(xt, nw, w_up, conv_w, conv_b, w_down, final_w, *, seq, tm, fc, final_norm):
    t, d = xt.shape
    d_ff = w_down.shape[0]
    n_chunks = d_ff // fc

    def chunked(m):
        r = m.shape[0]
        mu = m[:, :d_ff].reshape(r, n_chunks, fc)
        mg = m[:, d_ff:].reshape(r, n_chunks, fc)
        return jnp.transpose(jnp.concatenate([mu, mg], axis=2), (1, 0, 2))

    wup = chunked(w_up).astype(BF16)
    cw = chunked(conv_w)
    cb = chunked(conv_b.reshape(1, 2 * d_ff))
    wdn = w_down.reshape(n_chunks, fc, d).astype(BF16)
    nslab = 2 * fc // LANES
    body = functools.partial(_ffn_body, tm=tm, fc=fc, n_chunks=n_chunks, tiles_per_seq=seq // tm,
                             final_norm=final_norm)
    return pl.pallas_call(
        body,
        out_shape=jax.ShapeDtypeStruct((t, d), F32),
        grid=(t // tm,),
        in_specs=[
            pl.BlockSpec((tm, d), lambda i: (i, 0)),
            _const_spec((1, d)),
            _const_spec((n_chunks, d, 2 * fc)),
            _const_spec((n_chunks, FFN_CONV, 2 * fc)),
            _const_spec((n_chunks, 1, 2 * fc)),
            _const_spec((n_chunks, fc, d)),
            _const_spec((1, d)),
        ],
        out_specs=pl.BlockSpec((tm, d), lambda i: (i, 0)),
        scratch_shapes=[
            pltpu.VMEM((tm, d), BF16),
            pltpu.VMEM((nslab, SUBLANES + tm, LANES), F32),
            pltpu.VMEM((n_chunks, nslab, SUBLANES, LANES), F32),
        ],
        compiler_params=pltpu.CompilerParams(dimension_semantics=("arbitrary",),
                                             vmem_limit_bytes=VMEM_LIMIT_BYTES),
        name="conv_ffn_final" if final_norm else "conv_ffn",
    )(xt, nw.reshape(1, d), wup, cw, cb, wdn, final_w.reshape(1, d))


def kernel(x, positions, norm_mix, norm_ffn, norm_final, mix_w_in, pool_w, pool_scale, attn_sinks,
           mix_w_out, ssm_w_in, ssm_conv_w, ssm_conv_b, ssm_dt_bias, ssm_A_log, ssm_D, ssm_norm,
           ssm_w_out, ffn_w_up, ffn_conv_w, ffn_conv_b, ffn_w_down):
    b, s, d = x.shape
    depth = norm_mix.shape[0]
    xt = x.reshape(b * s, d)
    pos = positions.reshape(b * s, 1)
    for i in range(depth):
        j = i // 2
        if i % 2 == 0:
            xt = _mix0_layer(xt, pos, norm_mix[i], mix_w_in[j], pool_w[j], pool_scale[j], attn_sinks[j],
                             mix_w_out[j], seq=s, tm=512)
        else:
            xt = _ssd_layer(xt, norm_mix[i], ssm_w_in[j], ssm_conv_w[j], ssm_conv_b[j], ssm_dt_bias[j],
                            ssm_A_log[j], ssm_D[j], ssm_norm[j], ssm_w_out[j], seq=s, tm=256)
        xt = _ffn_layer(xt, norm_ffn[i], ffn_w_up[i], ffn_conv_w[i], ffn_conv_b[i], ffn_w_down[i],
                        norm_final, seq=s, tm=512, fc=256, final_norm=(i == depth - 1))
    return xt.reshape(b, s, d)
```

```python
import functools
import math

import numpy as np
import jax
import jax.numpy as jnp
from jax import lax
from jax.experimental import pallas as pl
from jax.experimental.pallas import tpu as pltpu

HEAD_DIM = 64
N_HEADS = 8
N_KV_HEADS = 2
GQ = N_HEADS // N_KV_HEADS
ATT_BLOCK = 128
ROPE_THETA = 10000.0
POOL_WINDOWS = (2, 4, 8, 16)
SSM_HEAD_DIM = 64
SSM_HEADS = 32
SSM_GROUPS = 8
SSM_STATE = 128
SSM_CONV = 4
SSM_CHUNK = 128
FFN_CONV = 3
NORM_EPS = 1e-6
SSM_NORM_EPS = 1e-5

LANES = 128
SUBLANES = 8
VMEM_LIMIT_BYTES = 56 * 1024 * 1024

F32 = jnp.float32
BF16 = jnp.bfloat16


def _rms(x, w, eps):
    ms = jnp.mean(x * x, axis=-1, keepdims=True)
    return x * lax.rsqrt(ms + eps) * w


def _silu(x):
    return x * jax.nn.sigmoid(x)


def _const_spec(shape):
    nd = len(shape)
    return pl.BlockSpec(shape, lambda i: (0,) * nd, pipeline_mode=pl.Buffered(1))


def _split3_f32(x):
    hi = x.astype(BF16).astype(F32)
    r1 = x - hi
    mid = r1.astype(BF16).astype(F32)
    lo = (r1 - mid).astype(BF16).astype(F32)
    return hi, mid, lo


def _mix0_body(x_ref, pos_ref, nw_ref, win_ref, invf_ref, sgn_ref, poolw_ref, pscale_ref,
               sink_ref, wout_ref, o_ref, ubuf, kbuf, vbuf, qbuf, mixbuf, *, tm, tiles_per_seq):
    i = pl.program_id(0)
    tile_in_seq = lax.rem(i, tiles_per_seq)
    nblk = tm // ATT_BLOCK
    halo = max(POOL_WINDOWS)
    pool_dim = LANES * len(POOL_WINDOWS)
    q_dim = N_HEADS * HEAD_DIM
    nslab = q_dim // LANES

    @pl.when(tile_in_seq == 0)
    def _():
        ubuf[:, 0:halo, :] = jnp.zeros((len(POOL_WINDOWS), halo, LANES), F32)
        kbuf[0:ATT_BLOCK, :] = jnp.zeros((ATT_BLOCK, LANES), BF16)
        vbuf[0:ATT_BLOCK, :] = jnp.zeros((ATT_BLOCK, LANES), BF16)

    x = x_ref[...]
    hb = _rms(x, nw_ref[...], NORM_EPS).astype(BF16)
    proj = jnp.dot(hb, win_ref[...], preferred_element_type=F32)

    ang = pos_ref[...].astype(F32) * invf_ref[...]
    cs = jnp.cos(ang)
    sn = jnp.sin(ang) * sgn_ref[...]
    lane = lax.broadcasted_iota(jnp.int32, (tm, LANES), 1)
    first_half = (lane & (HEAD_DIM // 2)) == 0
    low_head = lane < HEAD_DIM

    def rope(t):
        up = pltpu.roll(t, LANES - HEAD_DIM // 2, 1)
        dn = pltpu.roll(t, HEAD_DIM // 2, 1)
        return t * cs + jnp.where(first_half, up, dn) * sn

    scale = HEAD_DIM ** -0.5
    for j in range(nslab):
        q = rope(proj[:, pool_dim + LANES * j: pool_dim + LANES * (j + 1)]) * scale
        qa = jnp.where(low_head, q, 0.0).astype(BF16)
        qb = jnp.where(low_head, 0.0, q).astype(BF16)
        for b in range(nblk):
            rows = slice(ATT_BLOCK * b, ATT_BLOCK * (b + 1))
            qbuf[j, b, 0:ATT_BLOCK, :] = qa[rows]
            qbuf[j, b, ATT_BLOCK:2 * ATT_BLOCK, :] = qb[rows]
    kcol = pool_dim + q_dim
    kbuf[ATT_BLOCK:ATT_BLOCK + tm, :] = rope(proj[:, kcol:kcol + LANES]).astype(BF16)
    vbuf[ATT_BLOCK:ATT_BLOCK + tm, :] = proj[:, kcol + LANES:kcol + 2 * LANES].astype(BF16)

    tpos = tile_in_seq * tm + lax.broadcasted_iota(jnp.int32, (tm, 1), 0)
    for g, w in enumerate(POOL_WINDOWS):
        u_g = proj[:, LANES * g:LANES * (g + 1)]
        ubuf[g, halo:halo + tm, :] = u_g
        acc = u_g
        for k in range(1, w):
            acc = acc + ubuf[g, halo - k:halo - k + tm, :]
        cnt = jnp.minimum(tpos + 1, w).astype(F32)
        pooled = acc / cnt - u_g
        pm = jnp.dot(pooled.astype(BF16), poolw_ref[g], preferred_element_type=F32)
        mixbuf[:, LANES * g:LANES * (g + 1)] = (pm * pscale_ref[:, LANES * g:LANES * (g + 1)]).astype(BF16)
        ubuf[g, 0:halo, :] = ubuf[g, tm:tm + halo, :]

    qi = lax.broadcasted_iota(jnp.int32, (2 * ATT_BLOCK, 2 * ATT_BLOCK), 0) & (ATT_BLOCK - 1)
    kj = lax.broadcasted_iota(jnp.int32, (2 * ATT_BLOCK, 2 * ATT_BLOCK), 1)
    rel = qi + ATT_BLOCK - kj
    band = (rel >= 0) & (rel < ATT_BLOCK)
    first_lim = jnp.where(tile_in_seq == 0, ATT_BLOCK, 0)
    row2 = lax.broadcasted_iota(jnp.int32, (2 * ATT_BLOCK, 1), 0)
    lane_o = lax.broadcasted_iota(jnp.int32, (ATT_BLOCK, LANES), 1)
    for b in range(nblk):
        kk = kbuf[ATT_BLOCK * b:ATT_BLOCK * (b + 2), :]
        vv = vbuf[ATT_BLOCK * b:ATT_BLOCK * (b + 2), :]
        mask = (band & (kj >= first_lim)) if b == 0 else band
        for j in range(nslab):
            s = lax.dot_general(qbuf[j, b], kk, (((1,), (1,)), ((), ())),
                                preferred_element_type=F32)
            s = jnp.where(mask, s, -jnp.inf)
            sink = jnp.where(row2 < ATT_BLOCK, sink_ref[j], sink_ref[j + GQ])
            m = jnp.maximum(jnp.max(s, axis=-1, keepdims=True), sink)
            e = jnp.exp(s - m)
            den = jnp.sum(e, axis=-1, keepdims=True) + jnp.exp(sink - m)
            pv = jnp.dot(e.astype(BF16), vv, preferred_element_type=F32) / den
            o = jnp.where(lane_o < HEAD_DIM, pv[0:ATT_BLOCK], pv[ATT_BLOCK:2 * ATT_BLOCK])
            mixbuf[ATT_BLOCK * b:ATT_BLOCK * (b + 1),
                   pool_dim + LANES * j:pool_dim + LANES * (j + 1)] = o.astype(BF16)
    kbuf[0:ATT_BLOCK, :] = kbuf[tm:tm + ATT_BLOCK, :]
    vbuf[0:ATT_BLOCK, :] = vbuf[tm:tm + ATT_BLOCK, :]

    o_ref[...] = x + jnp.dot(mixbuf[...], wout_ref[...], preferred_element_type=F32)


def _mix0_layer(xt, pos, nw, w_in, pool_w, pool_scale, sinks, w_out, *, seq, tm):
    t, d = xt.shape
    pool_dim = LANES * len(POOL_WINDOWS)
    q_dim = N_HEADS * HEAD_DIM
    nslab = q_dim // LANES
    wq = w_in[:, pool_dim:pool_dim + q_dim].reshape(d, N_KV_HEADS, GQ, HEAD_DIM)
    wq = jnp.transpose(wq, (0, 2, 1, 3)).reshape(d, q_dim)
    w_in_p = jnp.concatenate([w_in[:, :pool_dim], wq, w_in[:, pool_dim + q_dim:]], axis=1).astype(BF16)
    wo = w_out[pool_dim:].reshape(N_KV_HEADS, GQ, HEAD_DIM, d)
    wo = jnp.transpose(wo, (1, 0, 2, 3)).reshape(q_dim, d)
    w_out_p = jnp.concatenate([w_out[:pool_dim], wo], axis=0).astype(BF16)
    inv_freq = ROPE_THETA ** (-np.arange(0, HEAD_DIM, 2, dtype=np.float32) / HEAD_DIM)
    invf = jnp.asarray(np.tile(inv_freq, LANES // (HEAD_DIM // 2))[None, :], F32)
    sgn = jnp.asarray(np.where((np.arange(LANES) % HEAD_DIM) < HEAD_DIM // 2, -1.0, 1.0)[None, :], F32)
    mix_in = w_in_p.shape[1]
    nblk = tm // ATT_BLOCK
    halo = max(POOL_WINDOWS)
    body = functools.partial(_mix0_body, tm=tm, tiles_per_seq=seq // tm)
    return pl.pallas_call(
        body,
        out_shape=jax.ShapeDtypeStruct((t, d), F32),
        grid=(t // tm,),
        in_specs=[
            pl.BlockSpec((tm, d), lambda i: (i, 0)),
            pl.BlockSpec((tm, 1), lambda i: (i, 0)),
            _const_spec((1, d)),
            _const_spec((d, mix_in)),
            _const_spec((1, LANES)),
            _const_spec((1, LANES)),
            _const_spec(pool_w.shape),
            _const_spec((1, pool_dim)),
            pl.BlockSpec(memory_space=pltpu.SMEM),
            _const_spec((pool_dim + q_dim, d)),
        ],
        out_specs=pl.BlockSpec((tm, d), lambda i: (i, 0)),
        scratch_shapes=[
            pltpu.VMEM((len(POOL_WINDOWS), halo + tm, LANES), F32),
            pltpu.VMEM((ATT_BLOCK + tm, LANES), BF16),
            pltpu.VMEM((ATT_BLOCK + tm, LANES), BF16),
            pltpu.VMEM((nslab, nblk, 2 * ATT_BLOCK, LANES), BF16),
            pltpu.VMEM((tm, pool_dim + q_dim), BF16),
        ],
        compiler_params=pltpu.CompilerParams(dimension_semantics=("arbitrary",),
                                             vmem_limit_bytes=VMEM_LIMIT_BYTES),
        name="pool_swa_mixer",
    )(xt, pos, nw.reshape(1, d), w_in_p, invf, sgn, pool_w.astype(BF16),
      pool_scale.reshape(1, pool_dim), sinks, w_out_p)


def _ssd_body(x_ref, nw_ref, wz_ref, wxbc_ref, wdt_ref, cw_ref, cb_ref, dtb_ref, alog_ref,
              dvec_ref, gnw_ref, wout_ref, tril_ref, e3_ref, eh_ref, o_ref,
              hb_ref, s_ref, halo_ref, xs_ref, xsb_ref, bt_ref, c_ref, y_ref, state_ref,
              *, tm, tiles_per_seq):
    i = pl.program_id(0)
    tile_in_seq = lax.rem(i, tiles_per_seq)
    d_inner = SSM_HEADS * SSM_HEAD_DIM
    bc_dim = SSM_GROUPS * SSM_STATE
    n_xs = d_inner // LANES
    n_b = bc_dim // LANES
    n_conv_slabs = n_xs + 2 * n_b
    cw_cols = 4 * LANES
    slabs_per_chunk = cw_cols // LANES
    nchunk = tm // SSM_CHUNK
    L = SSM_CHUNK
    heads_per_group = SSM_HEADS // SSM_GROUPS
    gw = heads_per_group * SSM_HEAD_DIM

    @pl.when(tile_in_seq == 0)
    def _():
        halo_ref[...] = jnp.zeros(halo_ref.shape, halo_ref.dtype)
        state_ref[...] = jnp.zeros(state_ref.shape, state_ref.dtype)

    x = x_ref[...]
    hb_ref[...] = _rms(x, nw_ref[...], NORM_EPS).astype(BF16)

    for cc in range(n_conv_slabs // slabs_per_chunk):
        slot = cc % 2
        hid = jnp.dot(hb_ref[...], wxbc_ref[:, cw_cols * cc:cw_cols * (cc + 1)],
                      preferred_element_type=F32)
        for jj in range(slabs_per_chunk):
            j = slabs_per_chunk * cc + jj
            cols = slice(LANES * j, LANES * (j + 1))
            x0 = hid[:, LANES * jj:LANES * (jj + 1)]
            s_ref[slot, jj, 0:SUBLANES, :] = halo_ref[j]
            s_ref[slot, jj, SUBLANES:SUBLANES + tm, :] = x0
            halo_ref[j] = x0[tm - SUBLANES:tm]
            y = x0 * cw_ref[SSM_CONV - 1:SSM_CONV, cols] + cb_ref[:, cols]
            for k in range(1, SSM_CONV):
                y = y + (s_ref[slot, jj, SUBLANES - k:SUBLANES - k + tm, :]
                         * cw_ref[SSM_CONV - 1 - k:SSM_CONV - k, cols])
            y = _silu(y)
            if j < n_xs:
                xs_ref[j] = y
                xsb_ref[j] = y.astype(BF16)
            elif j < n_xs + n_b:
                for c in range(nchunk):
                    bt_ref[j - n_xs, c] = y[L * c:L * (c + 1)].T.astype(BF16)
            else:
                c_ref[j - n_xs - n_b] = y.astype(BF16)

    dt = jax.nn.softplus(jnp.dot(hb_ref[...], wdt_ref[...], preferred_element_type=F32) + dtb_ref[...])
    a = dt * (-jnp.exp(alog_ref[...]))
    lane = lax.broadcasted_iota(jnp.int32, (L, LANES), 1)

    def pack3(v):
        hi, mid, lo = _split3_f32(v)
        return jnp.where(lane < SSM_HEADS, hi, jnp.where(lane < 2 * SSM_HEADS, mid, lo)).astype(BF16)

    tril = tril_ref[...]
    li = lax.broadcasted_iota(jnp.int32, (L, L), 0)
    si = lax.broadcasted_iota(jnp.int32, (L, L), 1)
    causal = si <= li
    lane_x = lax.broadcasted_iota(jnp.int32, (L, LANES), 1)
    low_head = lane_x < SSM_HEAD_DIM

    for c in range(nchunk):
        rows = slice(L * c, L * (c + 1))
        a_c = a[rows]
        dt_c = dt[rows]
        hi, mid, lo = _split3_f32(a_c)
        a_cs = (jnp.dot(tril, hi.astype(BF16), preferred_element_type=F32)
                + jnp.dot(tril, mid.astype(BF16), preferred_element_type=F32)
                + jnp.dot(tril, lo.astype(BF16), preferred_element_type=F32))
        a_cst = a_cs.T
        dtt = dt_c.T
        a_last = a_cs[L - 1:L, :]
        wgt = dt_c * jnp.exp(a_last - a_cs)
        ea = jnp.exp(a_cs)
        acol = jnp.dot(pack3(a_cs), eh_ref[...], preferred_element_type=F32)
        wb = jnp.dot(pack3(wgt), e3_ref[...], preferred_element_type=F32)
        eab = jnp.dot(pack3(ea), e3_ref[...], preferred_element_type=F32)

        for g in range(SSM_GROUPS):
            gcols = slice(gw * g, gw * (g + 1))
            cc_ = c_ref[g, rows, :]
            bt = bt_ref[g, c]
            cb = jnp.dot(cc_, bt, preferred_element_type=F32)
            prev = state_ref[g]
            yoff = jnp.dot(cc_, prev.astype(BF16), preferred_element_type=F32) * eab[:, gcols]
            xs_g = jnp.concatenate([xs_ref[2 * g, rows, :], xs_ref[2 * g + 1, rows, :]], axis=1)
            xd = (xs_g * wb[:, gcols]).astype(BF16)
            st_new = jnp.dot(bt, xd, preferred_element_type=F32)
            state_ref[g] = prev * eab[L - 1:L, gcols] + st_new
            for pair in range(heads_per_group // 2):
                slab = 2 * g + pair
                ms = []
                for r in range(2):
                    h = heads_per_group * g + 2 * pair + r
                    seg = acol[:, LANES * h:LANES * (h + 1)] - a_cst[h:h + 1, :]
                    lm = jnp.exp(jnp.where(causal, seg, -jnp.inf))
                    ms.append((cb * lm * dtt[h:h + 1, :]).astype(BF16))
                mcat = jnp.concatenate(ms, axis=1)
                xpair = xsb_ref[slab, rows, :]
                zero = jnp.zeros_like(xpair)
                rhs = jnp.concatenate([jnp.where(low_head, xpair, zero),
                                       jnp.where(low_head, zero, xpair)], axis=0)
                yd = jnp.dot(mcat, rhs, preferred_element_type=F32)
                y_ref[slab, rows, :] = (yd + yoff[:, LANES * pair:LANES * (pair + 1)]
                                        + dvec_ref[slab:slab + 1, :] * xs_ref[slab, rows, :])

    ssq = jnp.zeros((tm, 1), F32)
    zc = 4 * LANES
    for cc in range(d_inner // zc):
        z = jnp.dot(hb_ref[...], wz_ref[:, zc * cc:zc * (cc + 1)], preferred_element_type=F32)
        for jj in range(zc // LANES):
            j = (zc // LANES) * cc + jj
            yg = y_ref[j] * _silu(z[:, LANES * jj:LANES * (jj + 1)])
            y_ref[j] = yg
            ssq = ssq + jnp.sum(yg * yg, axis=-1, keepdims=True)
    rinv = lax.rsqrt(ssq * (1.0 / d_inner) + SSM_NORM_EPS)
    yn = jnp.concatenate(
        [(y_ref[j] * rinv * gnw_ref[:, LANES * j:LANES * (j + 1)]).astype(BF16) for j in range(n_xs)],
        axis=1)
    o_ref[...] = x + jnp.dot(yn, wout_ref[...], preferred_element_type=F32)


def _ssd_layer(xt, nw, w_in, conv_w, conv_b, dt_bias, a_log, d_skip, norm_w, w_out, *, seq, tm):
    t, d = xt.shape
    d_inner = SSM_HEADS * SSM_HEAD_DIM
    bc_dim = SSM_GROUPS * SSM_STATE
    conv_dim = d_inner + 2 * bc_dim
    n_xs = d_inner // LANES
    n_b = bc_dim // LANES
    nchunk = tm // SSM_CHUNK
    L = SSM_CHUNK
    copies = 3
    pad = LANES - copies * SSM_HEADS

    def lanes3(v):
        return jnp.concatenate([v] * copies + [jnp.zeros(v.shape[:-1] + (pad,), v.dtype)], axis=-1)

    wz = w_in[:, :d_inner].astype(BF16)
    wxbc = w_in[:, d_inner:d_inner + conv_dim].astype(BF16)
    wdt = lanes3(w_in[:, d_inner + conv_dim:]).astype(BF16)
    dtb = lanes3(dt_bias.reshape(1, SSM_HEADS))
    alog = lanes3(a_log.reshape(1, SSM_HEADS))
    dvec = jnp.repeat(d_skip, SSM_HEAD_DIM).reshape(n_xs, LANES)
    tril = jnp.asarray(np.tril(np.ones((L, L), np.float32)), BF16)
    rows_h = np.arange(LANES) % SSM_HEADS
    valid = (np.arange(LANES) < copies * SSM_HEADS)[:, None]
    e3 = jnp.asarray(((rows_h[:, None] == (np.arange(d_inner) // SSM_HEAD_DIM)[None, :]) & valid)
                     .astype(np.float32), BF16)
    eh = jnp.asarray(((rows_h[:, None] == (np.arange(SSM_HEADS * LANES) // LANES)[None, :]) & valid)
                     .astype(np.float32), BF16)
    body = functools.partial(_ssd_body, tm=tm, tiles_per_seq=seq // tm)
    return pl.pallas_call(
        body,
        out_shape=jax.ShapeDtypeStruct((t, d), F32),
        grid=(t // tm,),
        in_specs=[
            pl.BlockSpec((tm, d), lambda i: (i, 0)),
            _const_spec((1, d)),
            _const_spec((d, d_inner)),
            _const_spec((d, conv_dim)),
            _const_spec((d, LANES)),
            _const_spec((SSM_CONV, conv_dim)),
            _const_spec((1, conv_dim)),
            _const_spec((1, LANES)),
            _const_spec((1, LANES)),
            _const_spec((n_xs, LANES)),
            _const_spec((1, d_inner)),
            _const_spec((d_inner, d)),
            _const_spec((L, L)),
            _const_spec((LANES, d_inner)),
            _const_spec((LANES, SSM_HEADS * LANES)),
        ],
        out_specs=pl.BlockSpec((tm, d), lambda i: (i, 0)),
        scratch_shapes=[
            pltpu.VMEM((tm, d), BF16),
            pltpu.VMEM((2, 4, SUBLANES + tm, LANES), F32),
            pltpu.VMEM((n_xs + 2 * n_b, SUBLANES, LANES), F32),
            pltpu.VMEM((n_xs, tm, LANES), F32),
            pltpu.VMEM((n_xs, tm, LANES), BF16),
            pltpu.VMEM((n_b, nchunk, SSM_STATE, L), BF16),
            pltpu.VMEM((n_b, tm, LANES), BF16),
            pltpu.VMEM((n_xs, tm, LANES), F32),
            pltpu.VMEM((SSM_GROUPS, SSM_STATE, 2 * LANES), F32),
        ],
        compiler_params=pltpu.CompilerParams(dimension_semantics=("arbitrary",),
                                             vmem_limit_bytes=VMEM_LIMIT_BYTES),
        name="ssd_mixer",
    )(xt, nw.reshape(1, d), wz, wxbc, wdt, conv_w, conv_b.reshape(1, conv_dim), dtb, alog, dvec,
      norm_w.reshape(1, d_inner), w_out.astype(BF16), tril, e3, eh)


def _ffn_body(x_ref, nw_ref, wu_ref, wg_ref, cw_ref, cb_ref, wdn_ref, fnw_ref, o_ref,
              hb_ref, s_ref, halo_ref, act_ref, *, tm, fc, tiles_per_seq, final_norm):
    i = pl.program_id(0)
    d_ff = act_ref.shape[1]
    half_slabs = fc // LANES

    @pl.when(lax.rem(i, tiles_per_seq) == 0)
    def _():
        halo_ref[...] = jnp.zeros(halo_ref.shape, halo_ref.dtype)

    x = x_ref[...]
    hb_ref[...] = _rms(x, nw_ref[...], NORM_EPS).astype(BF16)

    for c in range(d_ff // fc):
        slot = c % 2
        ys = []
        for half, w_ref in enumerate((wu_ref, wg_ref)):
            hid = jnp.dot(hb_ref[...], w_ref[:, fc * c:fc * (c + 1)], preferred_element_type=F32)
            for jj in range(half_slabs):
                j = half * half_slabs + jj
                col = half * d_ff + fc * c + LANES * jj
                gs = col // LANES
                x0 = hid[:, LANES * jj:LANES * (jj + 1)]
                s_ref[slot, j, 0:SUBLANES, :] = halo_ref[gs]
                s_ref[slot, j, SUBLANES:SUBLANES + tm, :] = x0
                halo_ref[gs] = x0[tm - SUBLANES:tm]
                y = x0 * cw_ref[FFN_CONV - 1:FFN_CONV, col:col + LANES] + cb_ref[:, col:col + LANES]
                for k in range(1, FFN_CONV):
                    y = y + (s_ref[slot, j, SUBLANES - k:SUBLANES - k + tm, :]
                             * cw_ref[FFN_CONV - 1 - k:FFN_CONV - k, col:col + LANES])
                ys.append(y)
        for jj in range(half_slabs):
            u = ys[jj]
            g = ys[half_slabs + jj]
            act_ref[:, fc * c + LANES * jj:fc * c + LANES * (jj + 1)] = (_silu(g) * u).astype(BF16)

    out = x + jnp.dot(act_ref[...], wdn_ref[...], preferred_element_type=F32)
    if final_norm:
        out = _rms(out, fnw_ref[...], NORM_EPS)
    o_ref[...] = out


def _ffn_layer(xt, nw, w_up, conv_w, conv_b, w_down, final_w, *, seq, tm, fc, final_norm):
    t, d = xt.shape
    d_ff = w_down.shape[0]
    wu = w_up[:, :d_ff].astype(BF16)
    wg = w_up[:, d_ff:].astype(BF16)
    body = functools.partial(_ffn_body, tm=tm, fc=fc, tiles_per_seq=seq // tm, final_norm=final_norm)
    return pl.pallas_call(
        body,
        out_shape=jax.ShapeDtypeStruct((t, d), F32),
        grid=(t // tm,),
        in_specs=[
            pl.BlockSpec((tm, d), lambda i: (i, 0)),
            _const_spec((1, d)),
            _const_spec((d, d_ff)),
            _const_spec((d, d_ff)),
            _const_spec((FFN_CONV, 2 * d_ff)),
            _const_spec((1, 2 * d_ff)),
            _const_spec((d_ff, d)),
            _const_spec((1, d)),
        ],
        out_specs=pl.BlockSpec((tm, d), lambda i: (i, 0)),
        scratch_shapes=[
            pltpu.VMEM((tm, d), BF16),
            pltpu.VMEM((2, 2 * fc // LANES, SUBLANES + tm, LANES), F32),
            pltpu.VMEM((2 * d_ff // LANES, SUBLANES, LANES), F32),
            pltpu.VMEM((tm, d_ff), BF16),
        ],
        compiler_params=pltpu.CompilerParams(dimension_semantics=("arbitrary",),
                                             vmem_limit_bytes=VMEM_LIMIT_BYTES),
        name="conv_ffn_final" if final_norm else "conv_ffn",
    )(xt, nw.reshape(1, d), wu, wg, conv_w, conv_b.reshape(1, 2 * d_ff), w_down.astype(BF16),
      final_w.reshape(1, d))


def kernel(x, positions, norm_mix, norm_ffn, norm_final, mix_w_in, pool_w, pool_scale, attn_sinks,
           mix_w_out, ssm_w_in, ssm_conv_w, ssm_conv_b, ssm_dt_bias, ssm_A_log, ssm_D, ssm_norm,
           ssm_w_out, ffn_w_up, ffn_conv_w, ffn_conv_b, ffn_w_down):
    b, s, d = x.shape
    depth = norm_mix.shape[0]
    xt = x.reshape(b * s, d)
    pos = positions.reshape(b * s, 1)
    for i in range(depth):
        j = i // 2
        if i % 2 == 0:
            xt = _mix0_layer(xt, pos, norm_mix[i], mix_w_in[j], pool_w[j], pool_scale[j], attn_sinks[j],
                             mix_w_out[j], seq=s, tm=512)
        else:
            xt = _ssd_layer(xt, norm_mix[i], ssm_w_in[j], ssm_conv_w[j], ssm_conv_b[j], ssm_dt_bias[j],
                            ssm_A_log[j], ssm_D[j], ssm_norm[j], ssm_w_out[j], seq=s, tm=256)
        xt = _ffn_layer(xt, norm_ffn[i], ffn_w_up[i], ffn_conv_w[i], ffn_conv_b[i], ffn_w_down[i],
                        norm_final, seq=s, tm=512, fc=256, final_norm=(i == depth - 1))
    return xt.reshape(b, s, d)
```

```python
import functools
import math

import numpy as np
import jax
import jax.numpy as jnp
from jax import lax
from jax.experimental import pallas as pl
from jax.experimental.pallas import tpu as pltpu

HEAD_DIM = 64
N_HEADS = 8
N_KV_HEADS = 2
GQ = N_HEADS // N_KV_HEADS
ATT_BLOCK = 128
ROPE_THETA = 10000.0
POOL_WINDOWS = (2, 4, 8, 16)
SSM_HEAD_DIM = 64
SSM_HEADS = 32
SSM_GROUPS = 8
SSM_STATE = 128
SSM_CONV = 4
SSM_CHUNK = 128
FFN_CONV = 3
NORM_EPS = 1e-6
SSM_NORM_EPS = 1e-5

LANES = 128
SUBLANES = 8
VMEM_LIMIT_BYTES = 56 * 1024 * 1024

F32 = jnp.float32
BF16 = jnp.bfloat16


def _rms(x, w, eps):
    ms = jnp.mean(x * x, axis=-1, keepdims=True)
    return x * lax.rsqrt(ms + eps) * w


def _silu(x):
    return x * jax.nn.sigmoid(x)


def _const_spec(shape):
    nd = len(shape)
    return pl.BlockSpec(shape, lambda i: (0,) * nd, pipeline_mode=pl.Buffered(1))


def _split3_f32(x):
    hi = x.astype(BF16).astype(F32)
    r1 = x - hi
    mid = r1.astype(BF16).astype(F32)
    lo = (r1 - mid).astype(BF16).astype(F32)
    return hi, mid, lo


def _mix0_body(x_ref, pos_ref, nw_ref, win_ref, invf_ref, sgn_ref, poolw_ref, pscale_ref,
               sink_ref, wout_ref, o_ref, ubuf, kbuf, vbuf, qbuf, mixbuf, *, tm, tiles_per_seq):
    i = pl.program_id(0)
    tile_in_seq = lax.rem(i, tiles_per_seq)
    nblk = tm // ATT_BLOCK
    halo = max(POOL_WINDOWS)
    pool_dim = LANES * len(POOL_WINDOWS)
    q_dim = N_HEADS * HEAD_DIM
    nslab = q_dim // LANES

    @pl.when(tile_in_seq == 0)
    def _():
        ubuf[:, 0:halo, :] = jnp.zeros((len(POOL_WINDOWS), halo, LANES), F32)
        kbuf[0:ATT_BLOCK, :] = jnp.zeros((ATT_BLOCK, LANES), BF16)
        vbuf[0:ATT_BLOCK, :] = jnp.zeros((ATT_BLOCK, LANES), BF16)

    x = x_ref[...]
    hb = _rms(x, nw_ref[...], NORM_EPS).astype(BF16)
    proj = jnp.dot(hb, win_ref[...], preferred_element_type=F32)

    nfreq = HEAD_DIM // 2
    ngrp = LANES // nfreq
    ang = pos_ref[...].astype(F32) * invf_ref[...]
    grp = lax.broadcasted_iota(jnp.int32, (tm // ngrp, LANES), 1) // nfreq

    def spread(dense):
        rolled = [dense] + [pltpu.roll(dense, nfreq * s, 1) for s in range(1, ngrp)]
        parts = []
        for k in range(ngrp):
            out = rolled[(-k) % ngrp]
            for g in range(1, ngrp):
                out = jnp.where(grp == g, rolled[(g - k) % ngrp], out)
            parts.append(out)
        return jnp.concatenate(parts, axis=0)

    cs = spread(jnp.cos(ang))
    sn = spread(jnp.sin(ang)) * sgn_ref[...]
    lane = lax.broadcasted_iota(jnp.int32, (tm, LANES), 1)
    first_half = (lane & (HEAD_DIM // 2)) == 0
    low_head = lane < HEAD_DIM

    def rope(t):
        up = pltpu.roll(t, LANES - HEAD_DIM // 2, 1)
        dn = pltpu.roll(t, HEAD_DIM // 2, 1)
        return t * cs + jnp.where(first_half, up, dn) * sn

    scale = HEAD_DIM ** -0.5
    for j in range(nslab):
        q = rope(proj[:, pool_dim + LANES * j: pool_dim + LANES * (j + 1)]) * scale
        qa = jnp.where(low_head, q, 0.0).astype(BF16)
        qb = jnp.where(low_head, 0.0, q).astype(BF16)
        for b in range(nblk):
            rows = slice(ATT_BLOCK * b, ATT_BLOCK * (b + 1))
            qbuf[j, b, 0:ATT_BLOCK, :] = qa[rows]
            qbuf[j, b, ATT_BLOCK:2 * ATT_BLOCK, :] = qb[rows]
    kcol = pool_dim + q_dim
    kbuf[ATT_BLOCK:ATT_BLOCK + tm, :] = rope(proj[:, kcol:kcol + LANES]).astype(BF16)
    vbuf[ATT_BLOCK:ATT_BLOCK + tm, :] = proj[:, kcol + LANES:kcol + 2 * LANES].astype(BF16)

    tpos = tile_in_seq * tm + lax.broadcasted_iota(jnp.int32, (tm, 1), 0)
    for g, w in enumerate(POOL_WINDOWS):
        u_g = proj[:, LANES * g:LANES * (g + 1)]
        ubuf[g, halo:halo + tm, :] = u_g
        acc = u_g
        for k in range(1, w):
            acc = acc + ubuf[g, halo - k:halo - k + tm, :]
        cnt = jnp.minimum(tpos + 1, w).astype(F32)
        pooled = acc / cnt - u_g
        pm = jnp.dot(pooled.astype(BF16), poolw_ref[g], preferred_element_type=F32)
        mixbuf[:, LANES * g:LANES * (g + 1)] = (pm * pscale_ref[:, LANES * g:LANES * (g + 1)]).astype(BF16)
        ubuf[g, 0:halo, :] = ubuf[g, tm:tm + halo, :]

    qi = lax.broadcasted_iota(jnp.int32, (2 * ATT_BLOCK, 2 * ATT_BLOCK), 0) & (ATT_BLOCK - 1)
    kj = lax.broadcasted_iota(jnp.int32, (2 * ATT_BLOCK, 2 * ATT_BLOCK), 1)
    rel = qi + ATT_BLOCK - kj
    band = (rel >= 0) & (rel < ATT_BLOCK)
    first_lim = jnp.where(tile_in_seq == 0, ATT_BLOCK, 0)
    row2 = lax.broadcasted_iota(jnp.int32, (2 * ATT_BLOCK, 1), 0)
    lane_o = lax.broadcasted_iota(jnp.int32, (ATT_BLOCK, LANES), 1)
    for b in range(nblk):
        kk = kbuf[ATT_BLOCK * b:ATT_BLOCK * (b + 2), :]
        vv = vbuf[ATT_BLOCK * b:ATT_BLOCK * (b + 2), :]
        mask = (band & (kj >= first_lim)) if b == 0 else band
        for j in range(nslab):
            s = lax.dot_general(qbuf[j, b], kk, (((1,), (1,)), ((), ())),
                                preferred_element_type=F32)
            s = jnp.where(mask, s, -jnp.inf)
            sink = jnp.where(row2 < ATT_BLOCK, sink_ref[j], sink_ref[j + GQ])
            m = jnp.maximum(jnp.max(s, axis=-1, keepdims=True), sink)
            e = jnp.exp(s - m)
            den = jnp.sum(e, axis=-1, keepdims=True) + jnp.exp(sink - m)
            pv = jnp.dot(e.astype(BF16), vv, preferred_element_type=F32) / den
            o = jnp.where(lane_o < HEAD_DIM, pv[0:ATT_BLOCK], pv[ATT_BLOCK:2 * ATT_BLOCK])
            mixbuf[ATT_BLOCK * b:ATT_BLOCK * (b + 1),
                   pool_dim + LANES * j:pool_dim + LANES * (j + 1)] = o.astype(BF16)
    kbuf[0:ATT_BLOCK, :] = kbuf[tm:tm + ATT_BLOCK, :]
    vbuf[0:ATT_BLOCK, :] = vbuf[tm:tm + ATT_BLOCK, :]

    o_ref[...] = x + jnp.dot(mixbuf[...], wout_ref[...], preferred_element_type=F32)


def _mix0_layer(xt, pos, nw, w_in, pool_w, pool_scale, sinks, w_out, *, seq, tm):
    t, d = xt.shape
    pool_dim = LANES * len(POOL_WINDOWS)
    q_dim = N_HEADS * HEAD_DIM
    nslab = q_dim // LANES
    wq = w_in[:, pool_dim:pool_dim + q_dim].reshape(d, N_KV_HEADS, GQ, HEAD_DIM)
    wq = jnp.transpose(wq, (0, 2, 1, 3)).reshape(d, q_dim)
    w_in_p = jnp.concatenate([w_in[:, :pool_dim], wq, w_in[:, pool_dim + q_dim:]], axis=1).astype(BF16)
    wo = w_out[pool_dim:].reshape(N_KV_HEADS, GQ, HEAD_DIM, d)
    wo = jnp.transpose(wo, (1, 0, 2, 3)).reshape(q_dim, d)
    w_out_p = jnp.concatenate([w_out[:pool_dim], wo], axis=0).astype(BF16)
    nfreq = HEAD_DIM // 2
    ngrp = LANES // nfreq
    inv_freq = ROPE_THETA ** (-jnp.arange(0, HEAD_DIM, 2, dtype=F32) / HEAD_DIM)
    invf = jnp.tile(inv_freq, ngrp)[None, :]
    sgn = jnp.asarray(np.where((np.arange(LANES) % HEAD_DIM) < nfreq, -1.0, 1.0)[None, :], F32)
    pos_d = jnp.transpose(pos.reshape(t // tm, ngrp, tm // ngrp), (0, 2, 1))
    pos_d = jnp.repeat(pos_d, nfreq, axis=2).reshape(t // ngrp, LANES)
    mix_in = w_in_p.shape[1]
    nblk = tm // ATT_BLOCK
    halo = max(POOL_WINDOWS)
    body = functools.partial(_mix0_body, tm=tm, tiles_per_seq=seq // tm)
    return pl.pallas_call(
        body,
        out_shape=jax.ShapeDtypeStruct((t, d), F32),
        grid=(t // tm,),
        in_specs=[
            pl.BlockSpec((tm, d), lambda i: (i, 0)),
            pl.BlockSpec((tm // ngrp, LANES), lambda i: (i, 0)),
            _const_spec((1, d)),
            _const_spec((d, mix_in)),
            _const_spec((1, LANES)),
            _const_spec((1, LANES)),
            _const_spec(pool_w.shape),
            _const_spec((1, pool_dim)),
            pl.BlockSpec(memory_space=pltpu.SMEM),
            _const_spec((pool_dim + q_dim, d)),
        ],
        out_specs=pl.BlockSpec((tm, d), lambda i: (i, 0)),
        scratch_shapes=[
            pltpu.VMEM((len(POOL_WINDOWS), halo + tm, LANES), F32),
            pltpu.VMEM((ATT_BLOCK + tm, LANES), BF16),
            pltpu.VMEM((ATT_BLOCK + tm, LANES), BF16),
            pltpu.VMEM((nslab, nblk, 2 * ATT_BLOCK, LANES), BF16),
            pltpu.VMEM((tm, pool_dim + q_dim), BF16),
        ],
        compiler_params=pltpu.CompilerParams(dimension_semantics=("arbitrary",),
                                             vmem_limit_bytes=VMEM_LIMIT_BYTES),
        name="pool_swa_mixer",
    )(xt, pos_d, nw.reshape(1, d), w_in_p, invf, sgn, pool_w.astype(BF16),
      pool_scale.reshape(1, pool_dim), sinks, w_out_p)


SSD_D_INNER = SSM_HEADS * SSM_HEAD_DIM
SSD_BC_DIM = SSM_GROUPS * SSM_STATE
SSD_N_XS = SSD_D_INNER // LANES
SSD_N_B = SSD_BC_DIM // LANES


def _ssd_stage_a(x_ref, nw_ref, wxbc_ref, wdt_ref, cw_ref, cb_ref, dtb_ref, alog_ref,
                 s_ref, halo_ref, buf, *, tm):
    hb_ref, xs_ref, xsb_ref, bt_ref, c_ref, dt_ref, a_ref = buf
    n_xs, n_b = SSD_N_XS, SSD_N_B
    n_conv_slabs = n_xs + 2 * n_b
    cw_cols = 4 * LANES
    slabs_per_chunk = cw_cols // LANES
    nchunk = tm // SSM_CHUNK
    L = SSM_CHUNK

    hb_ref[...] = _rms(x_ref[...], nw_ref[...], NORM_EPS).astype(BF16)

    n_cchunks = n_conv_slabs // slabs_per_chunk
    first_bc = n_xs // slabs_per_chunk
    for step, cc in enumerate(list(range(first_bc, n_cchunks)) + list(range(first_bc))):
        slot = step % 2
        hid = jnp.dot(hb_ref[...], wxbc_ref[:, cw_cols * cc:cw_cols * (cc + 1)],
                      preferred_element_type=F32)
        for jj in range(slabs_per_chunk):
            j = slabs_per_chunk * cc + jj
            cols = slice(LANES * j, LANES * (j + 1))
            x0 = hid[:, LANES * jj:LANES * (jj + 1)]
            s_ref[slot, jj, 0:SUBLANES, :] = halo_ref[j]
            s_ref[slot, jj, SUBLANES:SUBLANES + tm, :] = x0
            halo_ref[j] = x0[tm - SUBLANES:tm]
            y = x0 * cw_ref[SSM_CONV - 1:SSM_CONV, cols] + cb_ref[:, cols]
            for k in range(1, SSM_CONV):
                y = y + (s_ref[slot, jj, SUBLANES - k:SUBLANES - k + tm, :]
                         * cw_ref[SSM_CONV - 1 - k:SSM_CONV - k, cols])
            y = _silu(y)
            if j < n_xs:
                xs_ref[j] = y
                xsb_ref[j] = y.astype(BF16)
            elif j < n_xs + n_b:
                for c in range(nchunk):
                    bt_ref[j - n_xs, c] = y[L * c:L * (c + 1)].T.astype(BF16)
            else:
                c_ref[j - n_xs - n_b] = y.astype(BF16)

    dt = jax.nn.softplus(jnp.dot(hb_ref[...], wdt_ref[...], preferred_element_type=F32) + dtb_ref[...])
    dt_ref[...] = dt
    a_ref[...] = dt * (-jnp.exp(alog_ref[...]))


def _ssd_stage_b(x_ref, wz_ref, dvec_ref, gnw_ref, wout_ref, tril_ref, e3_ref, o_ref,
                 y_ref, state_ref, buf, *, tm):
    hb_ref, xs_ref, xsb_ref, bt_ref, c_ref, dt_ref, a_ref = buf
    d_inner, n_xs = SSD_D_INNER, SSD_N_XS
    nchunk = tm // SSM_CHUNK
    L = SSM_CHUNK
    heads_per_group = SSM_HEADS // SSM_GROUPS
    gw = heads_per_group * SSM_HEAD_DIM

    dt = dt_ref[...]
    a = a_ref[...]
    lane_t = lax.broadcasted_iota(jnp.int32, (tm, LANES), 1)
    hi, mid, lo = _split3_f32(dt)
    dt3 = jnp.where(lane_t < SSM_HEADS, hi, jnp.where(lane_t < 2 * SSM_HEADS, mid, lo)).astype(BF16)
    dtb = jnp.dot(dt3, e3_ref[...], preferred_element_type=F32)

    tril = tril_ref[...]
    li = lax.broadcasted_iota(jnp.int32, (L, L), 0)
    si = lax.broadcasted_iota(jnp.int32, (L, L), 1)
    causal = si <= li
    low_head = lax.broadcasted_iota(jnp.int32, (L, LANES), 1) < SSM_HEAD_DIM
    head_of_lane = lax.broadcasted_iota(jnp.int32, (L, gw), 1) // SSM_HEAD_DIM
    zero_blk = jnp.zeros((SSM_STATE, L), BF16)

    for c in range(nchunk):
        rows = slice(L * c, L * (c + 1))
        hi, mid, lo = _split3_f32(a[rows])
        a_cs = (jnp.dot(tril, hi.astype(BF16), preferred_element_type=F32)
                + jnp.dot(tril, mid.astype(BF16), preferred_element_type=F32)
                + jnp.dot(tril, lo.astype(BF16), preferred_element_type=F32))
        a_cst = a_cs.T
        dtt = dt[rows].T

        for gp in range(SSM_GROUPS // 2):
            g0 = 2 * gp
            ccat = jnp.concatenate([c_ref[g0, rows, :], c_ref[g0 + 1, rows, :]], axis=1)
            bd = jnp.concatenate(
                [jnp.concatenate([bt_ref[g0, c], zero_blk], axis=1),
                 jnp.concatenate([zero_blk, bt_ref[g0 + 1, c]], axis=1)], axis=0)
            cbp = jnp.dot(ccat, bd, preferred_element_type=F32)
            for gi in range(2):
                g = g0 + gi
                cb = cbp[:, L * gi:L * (gi + 1)]
                ms, eas, xds = [], [], []
                for pair in range(heads_per_group // 2):
                    slab = 2 * g + pair
                    acols = []
                    for r in range(2):
                        h = heads_per_group * g + 2 * pair + r
                        acol = jnp.broadcast_to(a_cs[:, h:h + 1], (L, LANES))
                        seg = acol - a_cst[h:h + 1, :]
                        lm = jnp.exp(jnp.where(causal, seg, -jnp.inf))
                        ms.append((cb * lm * dtt[h:h + 1, :]).astype(BF16))
                        acols.append(acol)
                    sel = jnp.where(low_head, acols[0], acols[1])
                    eas.append(jnp.exp(sel))
                    dec = jnp.exp(sel[L - 1:L, :] - sel)
                    xds.append((xs_ref[slab, rows, :] * dtb[rows, LANES * slab:LANES * (slab + 1)] * dec)
                               .astype(BF16))
                ea_g = jnp.concatenate(eas, axis=1)
                cc_ = c_ref[g, rows, :]
                prev = state_ref[g]
                yoff = jnp.dot(cc_, prev.astype(BF16), preferred_element_type=F32) * ea_g
                st_new = jnp.dot(bt_ref[g, c], jnp.concatenate(xds, axis=1), preferred_element_type=F32)
                state_ref[g] = prev * ea_g[L - 1:L, :] + st_new
                xg = jnp.concatenate([xsb_ref[2 * g, rows, :], xsb_ref[2 * g + 1, rows, :]], axis=1)
                zero_x = jnp.zeros_like(xg)
                rhs = jnp.concatenate([jnp.where(head_of_lane == r, xg, zero_x)
                                       for r in range(heads_per_group)], axis=0)
                yd = jnp.dot(jnp.concatenate(ms, axis=1), rhs, preferred_element_type=F32) + yoff
                for pair in range(heads_per_group // 2):
                    slab = 2 * g + pair
                    y_ref[slab, rows, :] = (yd[:, LANES * pair:LANES * (pair + 1)]
                                            + dvec_ref[slab:slab + 1, :] * xs_ref[slab, rows, :])

    ssq = jnp.zeros((tm, 1), F32)
    zc = 4 * LANES
    for cc in range(d_inner // zc):
        z = jnp.dot(hb_ref[...], wz_ref[:, zc * cc:zc * (cc + 1)], preferred_element_type=F32)
        for jj in range(zc // LANES):
            j = (zc // LANES) * cc + jj
            yg = y_ref[j] * _silu(z[:, LANES * jj:LANES * (jj + 1)])
            y_ref[j] = yg
            ssq = ssq + jnp.sum(yg * yg, axis=-1, keepdims=True)
    rinv = lax.rsqrt(ssq * (1.0 / d_inner) + SSM_NORM_EPS)
    yn = jnp.concatenate(
        [(y_ref[j] * rinv * gnw_ref[:, LANES * j:LANES * (j + 1)]).astype(BF16) for j in range(n_xs)],
        axis=1)
    o_ref[...] = x_ref[...] + jnp.dot(yn, wout_ref[...], preferred_element_type=F32)


def _ssd_body(xa_ref, xb_ref, nw_ref, wz_ref, wxbc_ref, wdt_ref, cw_ref, cb_ref, dtb_ref, alog_ref,
              dvec_ref, gnw_ref, wout_ref, tril_ref, e3_ref, o_ref,
              s_ref, halo_ref, y_ref, state_ref, *bufs, tm, tiles_per_seq):
    i = pl.program_id(0)
    nbuf = len(bufs) // 2
    buf_even, buf_odd = bufs[:nbuf], bufs[nbuf:]

    @pl.when(i == 0)
    def _():
        for r in buf_odd:
            r[...] = jnp.zeros(r.shape, r.dtype)

    @pl.when(lax.rem(i, tiles_per_seq) == 0)
    def _():
        halo_ref[...] = jnp.zeros(halo_ref.shape, halo_ref.dtype)

    @pl.when((i == 0) | (lax.rem(i - 1, tiles_per_seq) == 0))
    def _():
        state_ref[...] = jnp.zeros(state_ref.shape, state_ref.dtype)

    def both(buf_a, buf_b):
        _ssd_stage_a(xa_ref, nw_ref, wxbc_ref, wdt_ref, cw_ref, cb_ref, dtb_ref, alog_ref,
                     s_ref, halo_ref, buf_a, tm=tm)
        _ssd_stage_b(xb_ref, wz_ref, dvec_ref, gnw_ref, wout_ref, tril_ref, e3_ref, o_ref,
                     y_ref, state_ref, buf_b, tm=tm)

    @pl.when(lax.rem(i, 2) == 0)
    def _():
        both(buf_even, buf_odd)

    @pl.when(lax.rem(i, 2) == 1)
    def _():
        both(buf_odd, buf_even)


def _ssd_layer(xt, nw, w_in, conv_w, conv_b, dt_bias, a_log, d_skip, norm_w, w_out, *, seq, tm):
    t, d = xt.shape
    d_inner = SSM_HEADS * SSM_HEAD_DIM
    bc_dim = SSM_GROUPS * SSM_STATE
    conv_dim = d_inner + 2 * bc_dim
    n_xs = d_inner // LANES
    n_b = bc_dim // LANES
    nchunk = tm // SSM_CHUNK
    L = SSM_CHUNK
    copies = 3
    pad = LANES - copies * SSM_HEADS

    def lanes3(v):
        return jnp.concatenate([v] * copies + [jnp.zeros(v.shape[:-1] + (pad,), v.dtype)], axis=-1)

    wz = w_in[:, :d_inner].astype(BF16)
    wxbc = w_in[:, d_inner:d_inner + conv_dim].astype(BF16)
    wdt = lanes3(w_in[:, d_inner + conv_dim:]).astype(BF16)
    dtb = lanes3(dt_bias.reshape(1, SSM_HEADS))
    alog = lanes3(a_log.reshape(1, SSM_HEADS))
    dvec = jnp.repeat(d_skip, SSM_HEAD_DIM).reshape(n_xs, LANES)
    tril = jnp.asarray(np.tril(np.ones((L, L), np.float32)), BF16)
    rows_h = np.arange(LANES) % SSM_HEADS
    valid = (np.arange(LANES) < copies * SSM_HEADS)[:, None]
    e3 = jnp.asarray(((rows_h[:, None] == (np.arange(d_inner) // SSM_HEAD_DIM)[None, :]) & valid)
                     .astype(np.float32), BF16)
    n_tiles = t // tm
    body = functools.partial(_ssd_body, tm=tm, tiles_per_seq=seq // tm)
    staged = [
        pltpu.VMEM((tm, d), BF16),
        pltpu.VMEM((n_xs, tm, LANES), F32),
        pltpu.VMEM((n_xs, tm, LANES), BF16),
        pltpu.VMEM((n_b, nchunk, SSM_STATE, L), BF16),
        pltpu.VMEM((n_b, tm, LANES), BF16),
        pltpu.VMEM((tm, LANES), F32),
        pltpu.VMEM((tm, LANES), F32),
    ]
    return pl.pallas_call(
        body,
        out_shape=jax.ShapeDtypeStruct((t, d), F32),
        grid=(n_tiles + 1,),
        in_specs=[
            pl.BlockSpec((tm, d), lambda i: (jnp.minimum(i, n_tiles - 1), 0)),
            pl.BlockSpec((tm, d), lambda i: (jnp.maximum(i - 1, 0), 0)),
            _const_spec((1, d)),
            _const_spec((d, d_inner)),
            _const_spec((d, conv_dim)),
            _const_spec((d, LANES)),
            _const_spec((SSM_CONV, conv_dim)),
            _const_spec((1, conv_dim)),
            _const_spec((1, LANES)),
            _const_spec((1, LANES)),
            _const_spec((n_xs, LANES)),
            _const_spec((1, d_inner)),
            _const_spec((d_inner, d)),
            _const_spec((L, L)),
            _const_spec((LANES, d_inner)),
        ],
        out_specs=pl.BlockSpec((tm, d), lambda i: (jnp.maximum(i - 1, 0), 0)),
        scratch_shapes=[
            pltpu.VMEM((2, 4, SUBLANES + tm, LANES), F32),
            pltpu.VMEM((n_xs + 2 * n_b, SUBLANES, LANES), F32),
            pltpu.VMEM((n_xs, tm, LANES), F32),
            pltpu.VMEM((SSM_GROUPS, SSM_STATE, 2 * LANES), F32),
        ] + staged + staged,
        compiler_params=pltpu.CompilerParams(dimension_semantics=("arbitrary",),
                                             vmem_limit_bytes=VMEM_LIMIT_BYTES),
        name="ssd_mixer",
    )(xt, xt, nw.reshape(1, d), wz, wxbc, wdt, conv_w, conv_b.reshape(1, conv_dim), dtb, alog, dvec,
      norm_w.reshape(1, d_inner), w_out.astype(BF16), tril, e3)


def _ffn_body(x_ref, nw_ref, wu_ref, wg_ref, cw_ref, cb_ref, wdn_ref, fnw_ref, o_ref,
              hb_ref, s_ref, halo_ref, act_ref, *, tm, fc, tiles_per_seq, final_norm):
    i = pl.program_id(0)
    d_ff = act_ref.shape[1]
    half_slabs = fc // LANES

    @pl.when(lax.rem(i, tiles_per_seq) == 0)
    def _():
        halo_ref[...] = jnp.zeros(halo_ref.shape, halo_ref.dtype)

    x = x_ref[...]
    hb_ref[...] = _rms(x, nw_ref[...], NORM_EPS).astype(BF16)

    for c in range(d_ff // fc):
        slot = c % 2
        ys = []
        for half, w_ref in enumerate((wu_ref, wg_ref)):
            hid = jnp.dot(hb_ref[...], w_ref[:, fc * c:fc * (c + 1)], preferred_element_type=F32)
            for jj in range(half_slabs):
                j = half * half_slabs + jj
                col = half * d_ff + fc * c + LANES * jj
                gs = col // LANES
                x0 = hid[:, LANES * jj:LANES * (jj + 1)]
                s_ref[slot, j, 0:SUBLANES, :] = halo_ref[gs]
                s_ref[slot, j, SUBLANES:SUBLANES + tm, :] = x0
                halo_ref[gs] = x0[tm - SUBLANES:tm]
                y = x0 * cw_ref[FFN_CONV - 1:FFN_CONV, col:col + LANES] + cb_ref[:, col:col + LANES]
                for k in range(1, FFN_CONV):
                    y = y + (s_ref[slot, j, SUBLANES - k:SUBLANES - k + tm, :]
                             * cw_ref[FFN_CONV - 1 - k:FFN_CONV - k, col:col + LANES])
                ys.append(y)
        for jj in range(half_slabs):
            u = ys[jj]
            g = ys[half_slabs + jj]
            act_ref[:, fc * c + LANES * jj:fc * c + LANES * (jj + 1)] = (_silu(g) * u).astype(BF16)

    out = x + jnp.dot(act_ref[...], wdn_ref[...], preferred_element_type=F32)
    if final_norm:
        out = _rms(out, fnw_ref[...], NORM_EPS)
    o_ref[...] = out


def _ffn_layer(xt, nw, w_up, conv_w, conv_b, w_down, final_w, *, seq, tm, fc, final_norm):
    t, d = xt.shape
    d_ff = w_down.shape[0]
    wu = w_up[:, :d_ff].astype(BF16)
    wg = w_up[:, d_ff:].astype(BF16)
    body = functools.partial(_ffn_body, tm=tm, fc=fc, tiles_per_seq=seq // tm, final_norm=final_norm)
    return pl.pallas_call(
        body,
        out_shape=jax.ShapeDtypeStruct((t, d), F32),
        grid=(t // tm,),
        in_specs=[
            pl.BlockSpec((tm, d), lambda i: (i, 0)),
            _const_spec((1, d)),
            _const_spec((d, d_ff)),
            _const_spec((d, d_ff)),
            _const_spec((FFN_CONV, 2 * d_ff)),
            _const_spec((1, 2 * d_ff)),
            _const_spec((d_ff, d)),
            _const_spec((1, d)),
        ],
        out_specs=pl.BlockSpec((tm, d), lambda i: (i, 0)),
        scratch_shapes=[
            pltpu.VMEM((tm, d), BF16),
            pltpu.VMEM((2, 2 * fc // LANES, SUBLANES + tm, LANES), F32),
            pltpu.VMEM((2 * d_ff // LANES, SUBLANES, LANES), F32),
            pltpu.VMEM((tm, d_ff), BF16),
        ],
        compiler_params=pltpu.CompilerParams(dimension_semantics=("arbitrary",),
                                             vmem_limit_bytes=VMEM_LIMIT_BYTES),
        name="conv_ffn_final" if final_norm else "conv_ffn",
    )(xt, nw.reshape(1, d), wu, wg, conv_w, conv_b.reshape(1, 2 * d_ff), w_down.astype(BF16),
      final_w.reshape(1, d))


def kernel(x, positions, norm_mix, norm_ffn, norm_final, mix_w_in, pool_w, pool_scale, attn_sinks,
           mix_w_out, ssm_w_in, ssm_conv_w, ssm_conv_b, ssm_dt_bias, ssm_A_log, ssm_D, ssm_norm,
           ssm_w_out, ffn_w_up, ffn_conv_w, ffn_conv_b, ffn_w_down):
    b, s, d = x.shape
    depth = norm_mix.shape[0]
    xt = x.reshape(b * s, d)
    pos = positions.reshape(b * s)
    for i in range(depth):
        j = i // 2
        if i % 2 == 0:
            xt = _mix0_layer(xt, pos, norm_mix[i], mix_w_in[j], pool_w[j], pool_scale[j], attn_sinks[j],
                             mix_w_out[j], seq=s, tm=512)
        else:
            xt = _ssd_layer(xt, norm_mix[i], ssm_w_in[j], ssm_conv_w[j], ssm_conv_b[j], ssm_dt_bias[j],
                            ssm_A_log[j], ssm_D[j], ssm_norm[j], ssm_w_out[j], seq=s, tm=256)
        xt = _ffn_layer(xt, norm_ffn[i], ffn_w_up[i], ffn_conv_w[i], ffn_conv_b[i], ffn_w_down[i],
                        norm_final, seq=s, tm=512, fc=256, final_norm=(i == depth - 1))
    return xt.reshape(b, s, d)
```

```python
import functools
import math

import numpy as np
import jax
import jax.numpy as jnp
from jax import lax
from jax.experimental import pallas as pl
from jax.experimental.pallas import tpu as pltpu

HEAD_DIM = 64
N_HEADS = 8
N_KV_HEADS = 2
GQ = N_HEADS // N_KV_HEADS
ATT_BLOCK = 128
ROPE_THETA = 10000.0
POOL_WINDOWS = (2, 4, 8, 16)
SSM_HEAD_DIM = 64
SSM_HEADS = 32
SSM_GROUPS = 8
SSM_STATE = 128
SSM_CONV = 4
SSM_CHUNK = 128
FFN_CONV = 3
NORM_EPS = 1e-6
SSM_NORM_EPS = 1e-5

LANES = 128
SUBLANES = 8
VMEM_LIMIT_BYTES = 56 * 1024 * 1024

F32 = jnp.float32
BF16 = jnp.bfloat16


def _rms(x, w, eps):
    ms = jnp.mean(x * x, axis=-1, keepdims=True)
    return x * lax.rsqrt(ms + eps) * w


def _silu(x):
    return x * jax.nn.sigmoid(x)


def _const_spec(shape):
    nd = len(shape)
    return pl.BlockSpec(shape, lambda i: (0,) * nd, pipeline_mode=pl.Buffered(1))


def _stream_cast(src, dst_ref, stage_ref, sem, chunk_rows):
    rows = dst_ref.shape[0]
    n = rows // chunk_rows

    def copy(k):
        return pltpu.make_async_copy(src.at[pl.ds(k * chunk_rows, chunk_rows), :],
                                     stage_ref.at[k % 2], sem.at[k % 2])

    copy(0).start()
    for k in range(n):
        if k + 1 < n:
            copy(k + 1).start()
        copy(k).wait()
        dst_ref[chunk_rows * k:chunk_rows * (k + 1), :] = stage_ref[k % 2].astype(BF16)


def _split3_f32(x):
    hi = x.astype(BF16).astype(F32)
    r1 = x - hi
    mid = r1.astype(BF16).astype(F32)
    lo = (r1 - mid).astype(BF16).astype(F32)
    return hi, mid, lo


def _mix0_body(x_ref, pos_ref, nw_ref, win_ref, invf_ref, sgn_ref, poolw_ref, pscale_ref,
               sink_ref, wout_ref, o_ref, ubuf, kbuf, vbuf, qbuf, mixbuf, *, tm, tiles_per_seq):
    i = pl.program_id(0)
    tile_in_seq = lax.rem(i, tiles_per_seq)
    nblk = tm // ATT_BLOCK
    halo = max(POOL_WINDOWS)
    pool_dim = LANES * len(POOL_WINDOWS)
    q_dim = N_HEADS * HEAD_DIM
    nslab = q_dim // LANES

    @pl.when(tile_in_seq == 0)
    def _():
        ubuf[:, 0:halo, :] = jnp.zeros((len(POOL_WINDOWS), halo, LANES), F32)
        kbuf[0:ATT_BLOCK, :] = jnp.zeros((ATT_BLOCK, LANES), BF16)
        vbuf[0:ATT_BLOCK, :] = jnp.zeros((ATT_BLOCK, LANES), BF16)

    x = x_ref[...]
    hb = _rms(x, nw_ref[...], NORM_EPS).astype(BF16)
    proj = jnp.dot(hb, win_ref[...], preferred_element_type=F32)

    nfreq = HEAD_DIM // 2
    ngrp = LANES // nfreq
    ang = pos_ref[...].astype(F32) * invf_ref[...]
    grp = lax.broadcasted_iota(jnp.int32, (tm // ngrp, LANES), 1) // nfreq

    def spread(dense):
        rolled = [dense] + [pltpu.roll(dense, nfreq * s, 1) for s in range(1, ngrp)]
        parts = []
        for k in range(ngrp):
            out = rolled[(-k) % ngrp]
            for g in range(1, ngrp):
                out = jnp.where(grp == g, rolled[(g - k) % ngrp], out)
            parts.append(out)
        return jnp.concatenate(parts, axis=0)

    cs = spread(jnp.cos(ang))
    sn = spread(jnp.sin(ang)) * sgn_ref[...]
    lane = lax.broadcasted_iota(jnp.int32, (tm, LANES), 1)
    first_half = (lane & (HEAD_DIM // 2)) == 0
    low_head = lane < HEAD_DIM

    def rope(t):
        up = pltpu.roll(t, LANES - HEAD_DIM // 2, 1)
        dn = pltpu.roll(t, HEAD_DIM // 2, 1)
        return t * cs + jnp.where(first_half, up, dn) * sn

    scale = HEAD_DIM ** -0.5
    for j in range(nslab):
        q = rope(proj[:, pool_dim + LANES * j: pool_dim + LANES * (j + 1)]) * scale
        qa = jnp.where(low_head, q, 0.0).astype(BF16)
        qb = jnp.where(low_head, 0.0, q).astype(BF16)
        for b in range(nblk):
            rows = slice(ATT_BLOCK * b, ATT_BLOCK * (b + 1))
            qbuf[j, b, 0:ATT_BLOCK, :] = qa[rows]
            qbuf[j, b, ATT_BLOCK:2 * ATT_BLOCK, :] = qb[rows]
    kcol = pool_dim + q_dim
    kbuf[ATT_BLOCK:ATT_BLOCK + tm, :] = rope(proj[:, kcol:kcol + LANES]).astype(BF16)
    vbuf[ATT_BLOCK:ATT_BLOCK + tm, :] = proj[:, kcol + LANES:kcol + 2 * LANES].astype(BF16)

    tpos = tile_in_seq * tm + lax.broadcasted_iota(jnp.int32, (tm, 1), 0)
    for g, w in enumerate(POOL_WINDOWS):
        u_g = proj[:, LANES * g:LANES * (g + 1)]
        ubuf[g, halo:halo + tm, :] = u_g
        acc = u_g
        for k in range(1, w):
            acc = acc + ubuf[g, halo - k:halo - k + tm, :]
        cnt = jnp.minimum(tpos + 1, w).astype(F32)
        pooled = acc / cnt - u_g
        pm = jnp.dot(pooled.astype(BF16), poolw_ref[g], preferred_element_type=F32)
        mixbuf[:, LANES * g:LANES * (g + 1)] = (pm * pscale_ref[:, LANES * g:LANES * (g + 1)]).astype(BF16)
        ubuf[g, 0:halo, :] = ubuf[g, tm:tm + halo, :]

    qi = lax.broadcasted_iota(jnp.int32, (2 * ATT_BLOCK, 2 * ATT_BLOCK), 0) & (ATT_BLOCK - 1)
    kj = lax.broadcasted_iota(jnp.int32, (2 * ATT_BLOCK, 2 * ATT_BLOCK), 1)
    rel = qi + ATT_BLOCK - kj
    band = (rel >= 0) & (rel < ATT_BLOCK)
    first_lim = jnp.where(tile_in_seq == 0, ATT_BLOCK, 0)
    row2 = lax.broadcasted_iota(jnp.int32, (2 * ATT_BLOCK, 1), 0)
    lane_o = lax.broadcasted_iota(jnp.int32, (ATT_BLOCK, LANES), 1)
    for b in range(nblk):
        kk = kbuf[ATT_BLOCK * b:ATT_BLOCK * (b + 2), :]
        vv = vbuf[ATT_BLOCK * b:ATT_BLOCK * (b + 2), :]
        mask = (band & (kj >= first_lim)) if b == 0 else band
        for j in range(nslab):
            s = lax.dot_general(qbuf[j, b], kk, (((1,), (1,)), ((), ())),
                                preferred_element_type=F32)
            s = jnp.where(mask, s, -jnp.inf)
            sink = jnp.where(row2 < ATT_BLOCK, sink_ref[j], sink_ref[j + GQ])
            m = jnp.maximum(jnp.max(s, axis=-1, keepdims=True), sink)
            e = jnp.exp(s - m)
            den = jnp.sum(e, axis=-1, keepdims=True) + jnp.exp(sink - m)
            pv = jnp.dot(e.astype(BF16), vv, preferred_element_type=F32) / den
            o = jnp.where(lane_o < HEAD_DIM, pv[0:ATT_BLOCK], pv[ATT_BLOCK:2 * ATT_BLOCK])
            mixbuf[ATT_BLOCK * b:ATT_BLOCK * (b + 1),
                   pool_dim + LANES * j:pool_dim + LANES * (j + 1)] = o.astype(BF16)
    kbuf[0:ATT_BLOCK, :] = kbuf[tm:tm + ATT_BLOCK, :]
    vbuf[0:ATT_BLOCK, :] = vbuf[tm:tm + ATT_BLOCK, :]

    o_ref[...] = x + jnp.dot(mixbuf[...], wout_ref[...], preferred_element_type=F32)


def _mix0_layer(xt, pos, nw, w_in, pool_w, pool_scale, sinks, w_out, *, seq, tm):
    t, d = xt.shape
    pool_dim = LANES * len(POOL_WINDOWS)
    q_dim = N_HEADS * HEAD_DIM
    nslab = q_dim // LANES
    wq = w_in[:, pool_dim:pool_dim + q_dim].reshape(d, N_KV_HEADS, GQ, HEAD_DIM)
    wq = jnp.transpose(wq, (0, 2, 1, 3)).reshape(d, q_dim)
    w_in_p = jnp.concatenate([w_in[:, :pool_dim], wq, w_in[:, pool_dim + q_dim:]], axis=1).astype(BF16)
    wo = w_out[pool_dim:].reshape(N_KV_HEADS, GQ, HEAD_DIM, d)
    wo = jnp.transpose(wo, (1, 0, 2, 3)).reshape(q_dim, d)
    w_out_p = jnp.concatenate([w_out[:pool_dim], wo], axis=0).astype(BF16)
    nfreq = HEAD_DIM // 2
    ngrp = LANES // nfreq
    inv_freq = ROPE_THETA ** (-jnp.arange(0, HEAD_DIM, 2, dtype=F32) / HEAD_DIM)
    invf = jnp.tile(inv_freq, ngrp)[None, :]
    sgn = jnp.asarray(np.where((np.arange(LANES) % HEAD_DIM) < nfreq, -1.0, 1.0)[None, :], F32)
    pos_d = jnp.transpose(pos.reshape(t // tm, ngrp, tm // ngrp), (0, 2, 1))
    pos_d = jnp.repeat(pos_d, nfreq, axis=2).reshape(t // ngrp, LANES)
    mix_in = w_in_p.shape[1]
    nblk = tm // ATT_BLOCK
    halo = max(POOL_WINDOWS)
    body = functools.partial(_mix0_body, tm=tm, tiles_per_seq=seq // tm)
    return pl.pallas_call(
        body,
        out_shape=jax.ShapeDtypeStruct((t, d), F32),
        grid=(t // tm,),
        in_specs=[
            pl.BlockSpec((tm, d), lambda i: (i, 0)),
            pl.BlockSpec((tm // ngrp, LANES), lambda i: (i, 0)),
            _const_spec((1, d)),
            _const_spec((d, mix_in)),
            _const_spec((1, LANES)),
            _const_spec((1, LANES)),
            _const_spec(pool_w.shape),
            _const_spec((1, pool_dim)),
            pl.BlockSpec(memory_space=pltpu.SMEM),
            _const_spec((pool_dim + q_dim, d)),
        ],
        out_specs=pl.BlockSpec((tm, d), lambda i: (i, 0)),
        scratch_shapes=[
            pltpu.VMEM((len(POOL_WINDOWS), halo + tm, LANES), F32),
            pltpu.VMEM((ATT_BLOCK + tm, LANES), BF16),
            pltpu.VMEM((ATT_BLOCK + tm, LANES), BF16),
            pltpu.VMEM((nslab, nblk, 2 * ATT_BLOCK, LANES), BF16),
            pltpu.VMEM((tm, pool_dim + q_dim), BF16),
        ],
        compiler_params=pltpu.CompilerParams(dimension_semantics=("arbitrary",),
                                             vmem_limit_bytes=VMEM_LIMIT_BYTES),
        name="pool_swa_mixer",
    )(xt, pos_d, nw.reshape(1, d), w_in_p, invf, sgn, pool_w.astype(BF16),
      pool_scale.reshape(1, pool_dim), sinks, w_out_p)


SSD_D_INNER = SSM_HEADS * SSM_HEAD_DIM
SSD_BC_DIM = SSM_GROUPS * SSM_STATE
SSD_N_XS = SSD_D_INNER // LANES
SSD_N_B = SSD_BC_DIM // LANES


def _ssd_stage_a(x_ref, nw_ref, wxbc_ref, wdt_ref, cw_ref, cb_ref, dtb_ref, alog_ref,
                 s_ref, halo_ref, buf, *, tm):
    hb_ref, xs_ref, xsb_ref, bt_ref, c_ref, dt_ref, a_ref = buf
    n_xs, n_b = SSD_N_XS, SSD_N_B
    n_conv_slabs = n_xs + 2 * n_b
    cw_cols = 4 * LANES
    slabs_per_chunk = cw_cols // LANES
    nchunk = tm // SSM_CHUNK
    L = SSM_CHUNK

    hb_ref[...] = _rms(x_ref[...], nw_ref[...], NORM_EPS).astype(BF16)
    yield

    n_cchunks = n_conv_slabs // slabs_per_chunk
    first_bc = n_xs // slabs_per_chunk
    for step, cc in enumerate(list(range(first_bc, n_cchunks)) + list(range(first_bc))):
        slot = step % 2
        hid = jnp.dot(hb_ref[...], wxbc_ref[:, cw_cols * cc:cw_cols * (cc + 1)],
                      preferred_element_type=F32)
        for jj in range(slabs_per_chunk):
            j = slabs_per_chunk * cc + jj
            cols = slice(LANES * j, LANES * (j + 1))
            x0 = hid[:, LANES * jj:LANES * (jj + 1)]
            s_ref[slot, jj, 0:SUBLANES, :] = halo_ref[j]
            s_ref[slot, jj, SUBLANES:SUBLANES + tm, :] = x0
            halo_ref[j] = x0[tm - SUBLANES:tm]
            y = x0 * cw_ref[SSM_CONV - 1:SSM_CONV, cols] + cb_ref[:, cols]
            for k in range(1, SSM_CONV):
                y = y + (s_ref[slot, jj, SUBLANES - k:SUBLANES - k + tm, :]
                         * cw_ref[SSM_CONV - 1 - k:SSM_CONV - k, cols])
            y = _silu(y)
            if j < n_xs:
                xs_ref[j] = y
                xsb_ref[j] = y.astype(BF16)
            elif j < n_xs + n_b:
                for c in range(nchunk):
                    bt_ref[j - n_xs, c] = y[L * c:L * (c + 1)].T.astype(BF16)
            else:
                c_ref[j - n_xs - n_b] = y.astype(BF16)
        yield

    dt = jax.nn.softplus(jnp.dot(hb_ref[...], wdt_ref[...], preferred_element_type=F32) + dtb_ref[...])
    dt_ref[...] = dt
    a_ref[...] = dt * (-jnp.exp(alog_ref[...]))


def _ssd_stage_b(x_ref, wz_ref, dvec_ref, gnw_ref, wout_ref, tril_ref, e3_ref, o_ref,
                 y_ref, state_ref, buf, *, tm):
    hb_ref, xs_ref, xsb_ref, bt_ref, c_ref, dt_ref, a_ref = buf
    d_inner, n_xs = SSD_D_INNER, SSD_N_XS
    nchunk = tm // SSM_CHUNK
    L = SSM_CHUNK
    heads_per_group = SSM_HEADS // SSM_GROUPS
    gw = heads_per_group * SSM_HEAD_DIM

    dt = dt_ref[...]
    a = a_ref[...]
    lane_t = lax.broadcasted_iota(jnp.int32, (tm, LANES), 1)
    hi, mid, lo = _split3_f32(dt)
    dt3 = jnp.where(lane_t < SSM_HEADS, hi, jnp.where(lane_t < 2 * SSM_HEADS, mid, lo)).astype(BF16)
    dtb = jnp.dot(dt3, e3_ref[...], preferred_element_type=F32)
    yield

    tril = tril_ref[...]
    li = lax.broadcasted_iota(jnp.int32, (L, L), 0)
    si = lax.broadcasted_iota(jnp.int32, (L, L), 1)
    causal = si <= li
    low_head = lax.broadcasted_iota(jnp.int32, (L, LANES), 1) < SSM_HEAD_DIM
    head_of_lane = lax.broadcasted_iota(jnp.int32, (L, gw), 1) // SSM_HEAD_DIM
    zero_blk = jnp.zeros((SSM_STATE, L), BF16)

    for c in range(nchunk):
        rows = slice(L * c, L * (c + 1))
        hi, mid, lo = _split3_f32(a[rows])
        a_cs = (jnp.dot(tril, hi.astype(BF16), preferred_element_type=F32)
                + jnp.dot(tril, mid.astype(BF16), preferred_element_type=F32)
                + jnp.dot(tril, lo.astype(BF16), preferred_element_type=F32))
        a_cst = a_cs.T
        dtt = dt[rows].T

        for gp in range(SSM_GROUPS // 2):
            g0 = 2 * gp
            ccat = jnp.concatenate([c_ref[g0, rows, :], c_ref[g0 + 1, rows, :]], axis=1)
            bd = jnp.concatenate(
                [jnp.concatenate([bt_ref[g0, c], zero_blk], axis=1),
                 jnp.concatenate([zero_blk, bt_ref[g0 + 1, c]], axis=1)], axis=0)
            cbp = jnp.dot(ccat, bd, preferred_element_type=F32)
            for gi in range(2):
                g = g0 + gi
                cb = cbp[:, L * gi:L * (gi + 1)]
                ms, eas, xds = [], [], []
                for pair in range(heads_per_group // 2):
                    slab = 2 * g + pair
                    acols = []
                    for r in range(2):
                        h = heads_per_group * g + 2 * pair + r
                        acol = jnp.broadcast_to(a_cs[:, h:h + 1], (L, LANES))
                        seg = acol - a_cst[h:h + 1, :]
                        lm = jnp.exp(jnp.where(causal, seg, -jnp.inf))
                        ms.append((cb * lm * dtt[h:h + 1, :]).astype(BF16))
                        acols.append(acol)
                    sel = jnp.where(low_head, acols[0], acols[1])
                    eas.append(jnp.exp(sel))
                    dec = jnp.exp(sel[L - 1:L, :] - sel)
                    xds.append((xs_ref[slab, rows, :] * dtb[rows, LANES * slab:LANES * (slab + 1)] * dec)
                               .astype(BF16))
                ea_g = jnp.concatenate(eas, axis=1)
                cc_ = c_ref[g, rows, :]
                prev = state_ref[g]
                yoff = jnp.dot(cc_, prev.astype(BF16), preferred_element_type=F32) * ea_g
                st_new = jnp.dot(bt_ref[g, c], jnp.concatenate(xds, axis=1), preferred_element_type=F32)
                state_ref[g] = prev * ea_g[L - 1:L, :] + st_new
                xg = jnp.concatenate([xsb_ref[2 * g, rows, :], xsb_ref[2 * g + 1, rows, :]], axis=1)
                zero_x = jnp.zeros_like(xg)
                rhs = jnp.concatenate([jnp.where(head_of_lane == r, xg, zero_x)
                                       for r in range(heads_per_group)], axis=0)
                yd = jnp.dot(jnp.concatenate(ms, axis=1), rhs, preferred_element_type=F32) + yoff
                for pair in range(heads_per_group // 2):
                    slab = 2 * g + pair
                    y_ref[slab, rows, :] = (yd[:, LANES * pair:LANES * (pair + 1)]
                                            + dvec_ref[slab:slab + 1, :] * xs_ref[slab, rows, :])
            yield

    ssq = jnp.zeros((tm, 1), F32)
    zc = 4 * LANES
    for cc in range(d_inner // zc):
        z = jnp.dot(hb_ref[...], wz_ref[:, zc * cc:zc * (cc + 1)], preferred_element_type=F32)
        for jj in range(zc // LANES):
            j = (zc // LANES) * cc + jj
            yg = y_ref[j] * _silu(z[:, LANES * jj:LANES * (jj + 1)])
            y_ref[j] = yg
            ssq = ssq + jnp.sum(yg * yg, axis=-1, keepdims=True)
        yield
    rinv = lax.rsqrt(ssq * (1.0 / d_inner) + SSM_NORM_EPS)
    yn = jnp.concatenate(
        [(y_ref[j] * rinv * gnw_ref[:, LANES * j:LANES * (j + 1)]).astype(BF16) for j in range(n_xs)],
        axis=1)
    o_ref[...] = x_ref[...] + jnp.dot(yn, wout_ref[...], preferred_element_type=F32)


def _ssd_body(xa_ref, xb_ref, nw_ref, win_hbm, wdt_ref, cw_ref, cb_ref, dtb_ref, alog_ref,
              dvec_ref, gnw_ref, wout_hbm, tril_ref, e3_ref, o_ref,
              s_ref, halo_ref, y_ref, state_ref, wz_ref, wxbc_ref, wout_ref,
              z_stage, xbc_stage, out_stage, sem, *bufs, layer, tm, tiles_per_seq):
    i = pl.program_id(0)
    nbuf = len(bufs) // 2
    buf_even, buf_odd = bufs[:nbuf], bufs[nbuf:]

    @pl.when(i == 0)
    def _():
        for r in buf_odd:
            r[...] = jnp.zeros(r.shape, r.dtype)
        d_inner = wz_ref.shape[1]
        conv_dim = wxbc_ref.shape[1]
        _stream_cast(win_hbm.at[layer, :, pl.ds(0, d_inner)], wz_ref, z_stage, sem, z_stage.shape[1])
        _stream_cast(win_hbm.at[layer, :, pl.ds(d_inner, conv_dim)], wxbc_ref, xbc_stage, sem,
                     xbc_stage.shape[1])
        _stream_cast(wout_hbm.at[layer], wout_ref, out_stage, sem, out_stage.shape[1])

    @pl.when(lax.rem(i, tiles_per_seq) == 0)
    def _():
        halo_ref[...] = jnp.zeros(halo_ref.shape, halo_ref.dtype)

    @pl.when((i == 0) | (lax.rem(i - 1, tiles_per_seq) == 0))
    def _():
        state_ref[...] = jnp.zeros(state_ref.shape, state_ref.dtype)

    def both(buf_a, buf_b):
        stages = [
            _ssd_stage_a(xa_ref, nw_ref, wxbc_ref, wdt_ref, cw_ref, cb_ref, dtb_ref, alog_ref,
                         s_ref, halo_ref, buf_a, tm=tm),
            _ssd_stage_b(xb_ref, wz_ref, dvec_ref, gnw_ref, wout_ref, tril_ref, e3_ref, o_ref,
                         y_ref, state_ref, buf_b, tm=tm),
        ]
        while stages:
            for st in list(stages):
                if next(st, StopIteration) is StopIteration:
                    stages.remove(st)

    @pl.when(lax.rem(i, 2) == 0)
    def _():
        both(buf_even, buf_odd)

    @pl.when(lax.rem(i, 2) == 1)
    def _():
        both(buf_odd, buf_even)


def _ssd_layer(xt, nw, w_in_all, conv_w, conv_b, dt_bias, a_log, d_skip, norm_w, w_out_all, *, layer, seq, tm):
    t, d = xt.shape
    d_inner = SSM_HEADS * SSM_HEAD_DIM
    bc_dim = SSM_GROUPS * SSM_STATE
    conv_dim = d_inner + 2 * bc_dim
    n_xs = d_inner // LANES
    n_b = bc_dim // LANES
    nchunk = tm // SSM_CHUNK
    L = SSM_CHUNK
    copies = 3
    pad = LANES - copies * SSM_HEADS

    def lanes3(v):
        return jnp.concatenate([v] * copies + [jnp.zeros(v.shape[:-1] + (pad,), v.dtype)], axis=-1)

    wdt = lanes3(w_in_all[layer][:, d_inner + conv_dim:]).astype(BF16)
    dtb = lanes3(dt_bias.reshape(1, SSM_HEADS))
    alog = lanes3(a_log.reshape(1, SSM_HEADS))
    dvec = jnp.repeat(d_skip, SSM_HEAD_DIM).reshape(n_xs, LANES)
    tril = jnp.asarray(np.tril(np.ones((L, L), np.float32)), BF16)
    rows_h = np.arange(LANES) % SSM_HEADS
    valid = (np.arange(LANES) < copies * SSM_HEADS)[:, None]
    e3 = jnp.asarray(((rows_h[:, None] == (np.arange(d_inner) // SSM_HEAD_DIM)[None, :]) & valid)
                     .astype(np.float32), BF16)
    n_tiles = t // tm
    body = functools.partial(_ssd_body, layer=layer, tm=tm, tiles_per_seq=seq // tm)
    z_rows, xbc_rows, out_rows = 128, 64, 256
    staged = [
        pltpu.VMEM((tm, d), BF16),
        pltpu.VMEM((n_xs, tm, LANES), F32),
        pltpu.VMEM((n_xs, tm, LANES), BF16),
        pltpu.VMEM((n_b, nchunk, SSM_STATE, L), BF16),
        pltpu.VMEM((n_b, tm, LANES), BF16),
        pltpu.VMEM((tm, LANES), F32),
        pltpu.VMEM((tm, LANES), F32),
    ]
    return pl.pallas_call(
        body,
        out_shape=jax.ShapeDtypeStruct((t, d), F32),
        grid=(n_tiles + 1,),
        in_specs=[
            pl.BlockSpec((tm, d), lambda i: (jnp.minimum(i, n_tiles - 1), 0)),
            pl.BlockSpec((tm, d), lambda i: (jnp.maximum(i - 1, 0), 0)),
            _const_spec((1, d)),
            pl.BlockSpec(memory_space=pl.ANY),
            _const_spec((d, LANES)),
            _const_spec((SSM_CONV, conv_dim)),
            _const_spec((1, conv_dim)),
            _const_spec((1, LANES)),
            _const_spec((1, LANES)),
            _const_spec((n_xs, LANES)),
            _const_spec((1, d_inner)),
            pl.BlockSpec(memory_space=pl.ANY),
            _const_spec((L, L)),
            _const_spec((LANES, d_inner)),
        ],
        out_specs=pl.BlockSpec((tm, d), lambda i: (jnp.maximum(i - 1, 0), 0)),
        scratch_shapes=[
            pltpu.VMEM((2, 4, SUBLANES + tm, LANES), F32),
            pltpu.VMEM((n_xs + 2 * n_b, SUBLANES, LANES), F32),
            pltpu.VMEM((n_xs, tm, LANES), F32),
            pltpu.VMEM((SSM_GROUPS, SSM_STATE, 2 * LANES), F32),
            pltpu.VMEM((d, d_inner), BF16),
            pltpu.VMEM((d, conv_dim), BF16),
            pltpu.VMEM((d_inner, d), BF16),
            pltpu.VMEM((2, z_rows, d_inner), F32),
            pltpu.VMEM((2, xbc_rows, conv_dim), F32),
            pltpu.VMEM((2, out_rows, d), F32),
            pltpu.SemaphoreType.DMA((2,)),
        ] + staged + staged,
        compiler_params=pltpu.CompilerParams(dimension_semantics=("arbitrary",),
                                             vmem_limit_bytes=VMEM_LIMIT_BYTES),
        name="ssd_mixer",
    )(xt, xt, nw.reshape(1, d), w_in_all, wdt, conv_w, conv_b.reshape(1, conv_dim), dtb, alog, dvec,
      norm_w.reshape(1, d_inner), w_out_all, tril, e3)


def _ffn_body(x_ref, nw_ref, wup_hbm, cw_ref, cb_ref, wdn_hbm, fnw_ref, o_ref,
              hb_ref, s_ref, halo_ref, act_ref, wu_ref, wg_ref, wdn_ref,
              up_stage, dn_stage, sem, *, layer, tm, fc, tiles_per_seq, final_norm):
    i = pl.program_id(0)
    d_ff = act_ref.shape[1]
    half_slabs = fc // LANES

    @pl.when(i == 0)
    def _():
        _stream_cast(wup_hbm.at[layer, :, pl.ds(0, d_ff)], wu_ref, up_stage, sem, up_stage.shape[1])
        _stream_cast(wup_hbm.at[layer, :, pl.ds(d_ff, d_ff)], wg_ref, up_stage, sem, up_stage.shape[1])
        _stream_cast(wdn_hbm.at[layer], wdn_ref, dn_stage, sem, dn_stage.shape[1])

    @pl.when(lax.rem(i, tiles_per_seq) == 0)
    def _():
        halo_ref[...] = jnp.zeros(halo_ref.shape, halo_ref.dtype)

    x = x_ref[...]
    hb_ref[...] = _rms(x, nw_ref[...], NORM_EPS).astype(BF16)

    for c in range(d_ff // fc):
        slot = c % 2
        ys = []
        for half, w_ref in enumerate((wu_ref, wg_ref)):
            hid = jnp.dot(hb_ref[...], w_ref[:, fc * c:fc * (c + 1)], preferred_element_type=F32)
            for jj in range(half_slabs):
                j = half * half_slabs + jj
                col = half * d_ff + fc * c + LANES * jj
                gs = col // LANES
                x0 = hid[:, LANES * jj:LANES * (jj + 1)]
                s_ref[slot, j, 0:SUBLANES, :] = halo_ref[gs]
                s_ref[slot, j, SUBLANES:SUBLANES + tm, :] = x0
                halo_ref[gs] = x0[tm - SUBLANES:tm]
                y = x0 * cw_ref[FFN_CONV - 1:FFN_CONV, col:col + LANES] + cb_ref[:, col:col + LANES]
                for k in range(1, FFN_CONV):
                    y = y + (s_ref[slot, j, SUBLANES - k:SUBLANES - k + tm, :]
                             * cw_ref[FFN_CONV - 1 - k:FFN_CONV - k, col:col + LANES])
                ys.append(y)
        for jj in range(half_slabs):
            u = ys[jj]
            g = ys[half_slabs + jj]
            act_ref[:, fc * c + LANES * jj:fc * c + LANES * (jj + 1)] = (_silu(g) * u).astype(BF16)

    out = x + jnp.dot(act_ref[...], wdn_ref[...], preferred_element_type=F32)
    if final_norm:
        out = _rms(out, fnw_ref[...], NORM_EPS)
    o_ref[...] = out


def _ffn_layer(xt, nw, w_up_all, conv_w, conv_b, w_down_all, final_w, *, layer, seq, tm, fc, final_norm):
    t, d = xt.shape
    d_ff = w_down_all.shape[1]
    n_tiles = t // tm
    up_rows, dn_rows = 128, 256
    body = functools.partial(_ffn_body, layer=layer, tm=tm, fc=fc, tiles_per_seq=seq // tm,
                             final_norm=final_norm)
    return pl.pallas_call(
        body,
        out_shape=jax.ShapeDtypeStruct((t, d), F32),
        grid=(n_tiles,),
        in_specs=[
            pl.BlockSpec((tm, d), lambda i: (i, 0)),
            _const_spec((1, d)),
            pl.BlockSpec(memory_space=pl.ANY),
            _const_spec((FFN_CONV, 2 * d_ff)),
            _const_spec((1, 2 * d_ff)),
            pl.BlockSpec(memory_space=pl.ANY),
            _const_spec((1, d)),
        ],
        out_specs=pl.BlockSpec((tm, d), lambda i: (i, 0)),
        scratch_shapes=[
            pltpu.VMEM((tm, d), BF16),
            pltpu.VMEM((2, 2 * fc // LANES, SUBLANES + tm, LANES), F32),
            pltpu.VMEM((2 * d_ff // LANES, SUBLANES, LANES), F32),
            pltpu.VMEM((tm, d_ff), BF16),
            pltpu.VMEM((d, d_ff), BF16),
            pltpu.VMEM((d, d_ff), BF16),
            pltpu.VMEM((d_ff, d), BF16),
            pltpu.VMEM((2, up_rows, d_ff), F32),
            pltpu.VMEM((2, dn_rows, d), F32),
            pltpu.SemaphoreType.DMA((2,)),
        ],
        compiler_params=pltpu.CompilerParams(dimension_semantics=("arbitrary",),
                                             vmem_limit_bytes=VMEM_LIMIT_BYTES),
        name="conv_ffn_final" if final_norm else "conv_ffn",
    )(xt, nw.reshape(1, d), w_up_all, conv_w, conv_b.reshape(1, 2 * d_ff), w_down_all,
      final_w.reshape(1, d))


def kernel(x, positions, norm_mix, norm_ffn, norm_final, mix_w_in, pool_w, pool_scale, attn_sinks,
           mix_w_out, ssm_w_in, ssm_conv_w, ssm_conv_b, ssm_dt_bias, ssm_A_log, ssm_D, ssm_norm,
           ssm_w_out, ffn_w_up, ffn_conv_w, ffn_conv_b, ffn_w_down):
    b, s, d = x.shape
    depth = norm_mix.shape[0]
    xt = x.reshape(b * s, d)
    pos = positions.reshape(b * s)
    for i in range(depth):
        j = i // 2
        if i % 2 == 0:
            xt = _mix0_layer(xt, pos, norm_mix[i], mix_w_in[j], pool_w[j], pool_scale[j], attn_sinks[j],
                             mix_w_out[j], seq=s, tm=512)
        else:
            xt = _ssd_layer(xt, norm_mix[i], ssm_w_in, ssm_conv_w[j], ssm_conv_b[j], ssm_dt_bias[j],
                            ssm_A_log[j], ssm_D[j], ssm_norm[j], ssm_w_out, layer=j, seq=s, tm=256)
        xt = _ffn_layer(xt, norm_ffn[i], ffn_w_up, ffn_conv_w[i], ffn_conv_b[i], ffn_w_down,
                        norm_final, layer=i, seq=s, tm=512, fc=256, final_norm=(i == depth - 1))
    return xt.reshape(b, s, d)
```

```python
import functools
import math

import numpy as np
import jax
import jax.numpy as jnp
from jax import lax
from jax.experimental import pallas as pl
from jax.experimental.pallas import tpu as pltpu

HEAD_DIM = 64
N_HEADS = 8
N_KV_HEADS = 2
GQ = N_HEADS // N_KV_HEADS
ATT_BLOCK = 128
ROPE_THETA = 10000.0
POOL_WINDOWS = (2, 4, 8, 16)
SSM_HEAD_DIM = 64
SSM_HEADS = 32
SSM_GROUPS = 8
SSM_STATE = 128
SSM_CONV = 4
SSM_CHUNK = 128
FFN_CONV = 3
NORM_EPS = 1e-6
SSM_NORM_EPS = 1e-5

LANES = 128
SUBLANES = 8
VMEM_LIMIT_BYTES = 56 * 1024 * 1024

F32 = jnp.float32
BF16 = jnp.bfloat16


def _rms(x, w, eps):
    ms = jnp.mean(x * x, axis=-1, keepdims=True)
    return x * lax.rsqrt(ms + eps) * w


def _silu(x):
    return x * jax.nn.sigmoid(x)


def _const_spec(shape):
    nd = len(shape)
    return pl.BlockSpec(shape, lambda i: (0,) * nd, pipeline_mode=pl.Buffered(1))


def _stream_cast(src, dst_ref, stage_ref, sem, chunk_rows, transpose=False):
    rows = dst_ref.shape[1] if transpose else dst_ref.shape[0]
    n = rows // chunk_rows
    slots = stage_ref.shape[0]

    def copy(k):
        return pltpu.make_async_copy(src.at[pl.ds(k * chunk_rows, chunk_rows), :],
                                     stage_ref.at[k % slots], sem.at[k % slots])

    for k in range(min(slots - 1, n)):
        copy(k).start()
    for k in range(n):
        if k + slots - 1 < n:
            copy(k + slots - 1).start()
        copy(k).wait()
        if transpose:
            dst_ref[:, chunk_rows * k:chunk_rows * (k + 1)] = stage_ref[k % slots].T.astype(BF16)
        else:
            dst_ref[chunk_rows * k:chunk_rows * (k + 1), :] = stage_ref[k % slots].astype(BF16)


def _split3_f32(x):
    hi = x.astype(BF16).astype(F32)
    r1 = x - hi
    mid = r1.astype(BF16).astype(F32)
    lo = (r1 - mid).astype(BF16).astype(F32)
    return hi, mid, lo


MIX_POOL_DIM = LANES * len(POOL_WINDOWS)
MIX_Q_DIM = N_HEADS * HEAD_DIM
MIX_NSLAB = MIX_Q_DIM // LANES


def _mix0_stage_a(x_ref, pos_ref, nw_ref, win_ref, invf_ref, sgn_ref, poolw_ref, pscale_ref,
                  ubuf, buf, tile_in_seq, *, tm):
    qbuf, kt_ref, vt_ref, mixbuf = buf
    nblk = tm // ATT_BLOCK
    halo = max(POOL_WINDOWS)
    pool_dim, q_dim, nslab = MIX_POOL_DIM, MIX_Q_DIM, MIX_NSLAB

    hb = _rms(x_ref[...], nw_ref[...], NORM_EPS).astype(BF16)
    proj = jnp.dot(hb, win_ref[...], preferred_element_type=F32)
    yield

    nfreq = HEAD_DIM // 2
    ngrp = LANES // nfreq
    ang = pos_ref[...].astype(F32) * invf_ref[...]
    grp = lax.broadcasted_iota(jnp.int32, (tm // ngrp, LANES), 1) // nfreq

    def spread(dense):
        rolled = [dense] + [pltpu.roll(dense, nfreq * s, 1) for s in range(1, ngrp)]
        parts = []
        for k in range(ngrp):
            out = rolled[(-k) % ngrp]
            for g in range(1, ngrp):
                out = jnp.where(grp == g, rolled[(g - k) % ngrp], out)
            parts.append(out)
        return jnp.concatenate(parts, axis=0)

    cs = spread(jnp.cos(ang))
    sn = spread(jnp.sin(ang)) * sgn_ref[...]
    lane = lax.broadcasted_iota(jnp.int32, (tm, LANES), 1)
    first_half = (lane & (HEAD_DIM // 2)) == 0
    low_head = lane < HEAD_DIM

    def rope(t):
        up = pltpu.roll(t, LANES - HEAD_DIM // 2, 1)
        dn = pltpu.roll(t, HEAD_DIM // 2, 1)
        return t * cs + jnp.where(first_half, up, dn) * sn

    scale = HEAD_DIM ** -0.5
    for j in range(nslab):
        q = rope(proj[:, pool_dim + LANES * j: pool_dim + LANES * (j + 1)]) * scale
        qa = jnp.where(low_head, q, 0.0).astype(BF16)
        qb = jnp.where(low_head, 0.0, q).astype(BF16)
        for b in range(nblk):
            rows = slice(ATT_BLOCK * b, ATT_BLOCK * (b + 1))
            qbuf[j, b, 0:ATT_BLOCK, :] = qa[rows]
            qbuf[j, b, ATT_BLOCK:2 * ATT_BLOCK, :] = qb[rows]
        yield
    kcol = pool_dim + q_dim
    kr = rope(proj[:, kcol:kcol + LANES])
    for b in range(nblk):
        kt_ref[:, ATT_BLOCK * b:ATT_BLOCK * (b + 1)] = kr[ATT_BLOCK * b:ATT_BLOCK * (b + 1)].T.astype(BF16)
    vt_ref[...] = proj[:, kcol + LANES:kcol + 2 * LANES].astype(BF16)
    yield

    tpos = tile_in_seq * tm + lax.broadcasted_iota(jnp.int32, (tm, 1), 0)
    for g, w in enumerate(POOL_WINDOWS):
        u_g = proj[:, LANES * g:LANES * (g + 1)]
        ubuf[g, halo:halo + tm, :] = u_g
        acc = u_g
        for k in range(1, w):
            acc = acc + ubuf[g, halo - k:halo - k + tm, :]
        cnt = jnp.minimum(tpos + 1, w).astype(F32)
        pooled = acc / cnt - u_g
        pm = jnp.dot(pooled.astype(BF16), poolw_ref[g], preferred_element_type=F32)
        mixbuf[:, LANES * g:LANES * (g + 1)] = (pm * pscale_ref[:, LANES * g:LANES * (g + 1)]).astype(BF16)
        ubuf[g, 0:halo, :] = ubuf[g, tm:tm + halo, :]
        yield


def _mix0_stage_b(x_ref, sink_ref, wout_ref, o_ref, kcarry, vcarry, buf, tile_in_seq, *, tm):
    qbuf, kt_ref, vt_ref, mixbuf = buf
    nblk = tm // ATT_BLOCK
    pool_dim, nslab = MIX_POOL_DIM, MIX_NSLAB

    qi = lax.broadcasted_iota(jnp.int32, (2 * ATT_BLOCK, 2 * ATT_BLOCK), 0) & (ATT_BLOCK - 1)
    kj = lax.broadcasted_iota(jnp.int32, (2 * ATT_BLOCK, 2 * ATT_BLOCK), 1)
    rel = qi + ATT_BLOCK - kj
    band = (rel >= 0) & (rel < ATT_BLOCK)
    first_lim = jnp.where(tile_in_seq == 0, ATT_BLOCK, 0)
    row2 = lax.broadcasted_iota(jnp.int32, (2 * ATT_BLOCK, 1), 0)
    lane_o = lax.broadcasted_iota(jnp.int32, (ATT_BLOCK, LANES), 1)
    for b in range(nblk):
        if b == 0:
            kk = jnp.concatenate([kcarry[...], kt_ref[:, 0:ATT_BLOCK]], axis=1)
            vv = jnp.concatenate([vcarry[...], vt_ref[0:ATT_BLOCK, :]], axis=0)
        else:
            kk = kt_ref[:, ATT_BLOCK * (b - 1):ATT_BLOCK * (b + 1)]
            vv = vt_ref[ATT_BLOCK * (b - 1):ATT_BLOCK * (b + 1), :]
        mask = (band & (kj >= first_lim)) if b == 0 else band
        for j in range(nslab):
            s = jnp.dot(qbuf[j, b], kk, preferred_element_type=F32)
            s = jnp.where(mask, s, -jnp.inf)
            sink = jnp.where(row2 < ATT_BLOCK, sink_ref[j], sink_ref[j + GQ])
            m = jnp.maximum(jnp.max(s, axis=-1, keepdims=True), sink)
            e = jnp.exp(s - m)
            den = jnp.sum(e, axis=-1, keepdims=True) + jnp.exp(sink - m)
            pv = jnp.dot(e.astype(BF16), vv, preferred_element_type=F32) / den
            o = jnp.where(lane_o < HEAD_DIM, pv[0:ATT_BLOCK], pv[ATT_BLOCK:2 * ATT_BLOCK])
            mixbuf[ATT_BLOCK * b:ATT_BLOCK * (b + 1),
                   pool_dim + LANES * j:pool_dim + LANES * (j + 1)] = o.astype(BF16)
            yield
    kcarry[...] = kt_ref[:, tm - ATT_BLOCK:tm]
    vcarry[...] = vt_ref[tm - ATT_BLOCK:tm, :]

    o_ref[...] = x_ref[...] + jnp.dot(mixbuf[...], wout_ref[...], preferred_element_type=F32)


def _round_robin(stages):
    stages = list(stages)
    while stages:
        for st in list(stages):
            if next(st, StopIteration) is StopIteration:
                stages.remove(st)


def _mix0_body(xa_ref, xb_ref, pos_ref, nw_ref, win_ref, invf_ref, sgn_ref, poolw_ref, pscale_ref,
               sink_ref, wout_ref, o_ref, ubuf, kcarry, vcarry, *bufs, tm, tiles_per_seq):
    i = pl.program_id(0)
    nbuf = len(bufs) // 2
    buf_even, buf_odd = bufs[:nbuf], bufs[nbuf:]
    halo = max(POOL_WINDOWS)
    tile_a = lax.rem(i, tiles_per_seq)
    tile_b = lax.rem(i - 1, tiles_per_seq)

    @pl.when(i == 0)
    def _():
        for r in buf_odd + (kcarry, vcarry):
            r[...] = jnp.zeros(r.shape, r.dtype)

    @pl.when(tile_a == 0)
    def _():
        ubuf[:, 0:halo, :] = jnp.zeros((len(POOL_WINDOWS), halo, LANES), F32)

    def both(buf_a, buf_b):
        _round_robin([
            _mix0_stage_a(xa_ref, pos_ref, nw_ref, win_ref, invf_ref, sgn_ref, poolw_ref, pscale_ref,
                          ubuf, buf_a, tile_a, tm=tm),
            _mix0_stage_b(xb_ref, sink_ref, wout_ref, o_ref, kcarry, vcarry, buf_b, tile_b, tm=tm),
        ])

    @pl.when(lax.rem(i, 2) == 0)
    def _():
        both(buf_even, buf_odd)

    @pl.when(lax.rem(i, 2) == 1)
    def _():
        both(buf_odd, buf_even)


def _mix0_layer(xt, pos, nw, w_in, pool_w, pool_scale, sinks, w_out, *, seq, tm):
    t, d = xt.shape
    pool_dim = LANES * len(POOL_WINDOWS)
    q_dim = N_HEADS * HEAD_DIM
    nslab = q_dim // LANES
    wq = w_in[:, pool_dim:pool_dim + q_dim].reshape(d, N_KV_HEADS, GQ, HEAD_DIM)
    wq = jnp.transpose(wq, (0, 2, 1, 3)).reshape(d, q_dim)
    w_in_p = jnp.concatenate([w_in[:, :pool_dim], wq, w_in[:, pool_dim + q_dim:]], axis=1).astype(BF16)
    wo = w_out[pool_dim:].reshape(N_KV_HEADS, GQ, HEAD_DIM, d)
    wo = jnp.transpose(wo, (1, 0, 2, 3)).reshape(q_dim, d)
    w_out_p = jnp.concatenate([w_out[:pool_dim], wo], axis=0).astype(BF16)
    nfreq = HEAD_DIM // 2
    ngrp = LANES // nfreq
    inv_freq = ROPE_THETA ** (-jnp.arange(0, HEAD_DIM, 2, dtype=F32) / HEAD_DIM)
    invf = jnp.tile(inv_freq, ngrp)[None, :]
    sgn = jnp.asarray(np.where((np.arange(LANES) % HEAD_DIM) < nfreq, -1.0, 1.0)[None, :], F32)
    pos_d = jnp.transpose(pos.reshape(t // tm, ngrp, tm // ngrp), (0, 2, 1))
    pos_d = jnp.repeat(pos_d, nfreq, axis=2).reshape(t // ngrp, LANES)
    mix_in = w_in_p.shape[1]
    nblk = tm // ATT_BLOCK
    halo = max(POOL_WINDOWS)
    n_tiles = t // tm
    body = functools.partial(_mix0_body, tm=tm, tiles_per_seq=seq // tm)
    staged = [
        pltpu.VMEM((nslab, nblk, 2 * ATT_BLOCK, LANES), BF16),
        pltpu.VMEM((LANES, tm), BF16),
        pltpu.VMEM((tm, LANES), BF16),
        pltpu.VMEM((tm, pool_dim + q_dim), BF16),
    ]
    return pl.pallas_call(
        body,
        out_shape=jax.ShapeDtypeStruct((t, d), F32),
        grid=(n_tiles + 1,),
        in_specs=[
            pl.BlockSpec((tm, d), lambda i: (jnp.minimum(i, n_tiles - 1), 0)),
            pl.BlockSpec((tm, d), lambda i: (jnp.maximum(i - 1, 0), 0)),
            pl.BlockSpec((tm // ngrp, LANES), lambda i: (jnp.minimum(i, n_tiles - 1), 0)),
            _const_spec((1, d)),
            _const_spec((d, mix_in)),
            _const_spec((1, LANES)),
            _const_spec((1, LANES)),
            _const_spec(pool_w.shape),
            _const_spec((1, pool_dim)),
            pl.BlockSpec(memory_space=pltpu.SMEM),
            _const_spec((pool_dim + q_dim, d)),
        ],
        out_specs=pl.BlockSpec((tm, d), lambda i: (jnp.maximum(i - 1, 0), 0)),
        scratch_shapes=[
            pltpu.VMEM((len(POOL_WINDOWS), halo + tm, LANES), F32),
            pltpu.VMEM((ATT_BLOCK, LANES), BF16),
            pltpu.VMEM((ATT_BLOCK, LANES), BF16),
        ] + staged + staged,
        compiler_params=pltpu.CompilerParams(dimension_semantics=("arbitrary",),
                                             vmem_limit_bytes=VMEM_LIMIT_BYTES),
        name="pool_swa_mixer",
    )(xt, xt, pos_d, nw.reshape(1, d), w_in_p, invf, sgn, pool_w.astype(BF16),
      pool_scale.reshape(1, pool_dim), sinks, w_out_p)


SSD_D_INNER = SSM_HEADS * SSM_HEAD_DIM
SSD_BC_DIM = SSM_GROUPS * SSM_STATE
SSD_N_XS = SSD_D_INNER // LANES
SSD_N_B = SSD_BC_DIM // LANES


def _ssd_stage_a(x_ref, nw_ref, wxbc_ref, wdt_ref, cw_ref, cb_ref, dtb_ref, alog_ref,
                 s_ref, halo_ref, buf, *, tm):
    hb_ref, xs_ref, xsb_ref, bt_ref, c_ref, dt_ref, a_ref = buf
    n_xs, n_b = SSD_N_XS, SSD_N_B
    n_conv_slabs = n_xs + 2 * n_b
    cw_cols = 4 * LANES
    slabs_per_chunk = cw_cols // LANES
    nchunk = tm // SSM_CHUNK
    L = SSM_CHUNK

    hb_ref[...] = _rms(x_ref[...], nw_ref[...], NORM_EPS).astype(BF16)
    yield

    n_cchunks = n_conv_slabs // slabs_per_chunk
    first_bc = n_xs // slabs_per_chunk
    for step, cc in enumerate(list(range(first_bc, n_cchunks)) + list(range(first_bc))):
        slot = step % 2
        hid = jnp.dot(hb_ref[...], wxbc_ref[:, cw_cols * cc:cw_cols * (cc + 1)],
                      preferred_element_type=F32)
        for jj in range(slabs_per_chunk):
            j = slabs_per_chunk * cc + jj
            cols = slice(LANES * j, LANES * (j + 1))
            x0 = hid[:, LANES * jj:LANES * (jj + 1)]
            s_ref[slot, jj, 0:SUBLANES, :] = halo_ref[j]
            s_ref[slot, jj, SUBLANES:SUBLANES + tm, :] = x0
            halo_ref[j] = x0[tm - SUBLANES:tm]
            y = x0 * cw_ref[SSM_CONV - 1:SSM_CONV, cols] + cb_ref[:, cols]
            for k in range(1, SSM_CONV):
                y = y + (s_ref[slot, jj, SUBLANES - k:SUBLANES - k + tm, :]
                         * cw_ref[SSM_CONV - 1 - k:SSM_CONV - k, cols])
            y = _silu(y)
            if j < n_xs:
                xs_ref[j] = y
                xsb_ref[j] = y.astype(BF16)
            elif j < n_xs + n_b:
                for c in range(nchunk):
                    bt_ref[j - n_xs, c] = y[L * c:L * (c + 1)].T.astype(BF16)
            else:
                c_ref[j - n_xs - n_b] = y.astype(BF16)
        yield

    dt = jax.nn.softplus(jnp.dot(hb_ref[...], wdt_ref[...], preferred_element_type=F32) + dtb_ref[...])
    dt_ref[...] = dt
    a_ref[...] = dt * (-jnp.exp(alog_ref[...]))


def _ssd_stage_b(x_ref, wz_ref, dvec_ref, gnw_ref, wout_ref, tril_ref, e3_ref, o_ref,
                 y_ref, state_ref, buf, *, tm):
    hb_ref, xs_ref, xsb_ref, bt_ref, c_ref, dt_ref, a_ref = buf
    d_inner, n_xs = SSD_D_INNER, SSD_N_XS
    nchunk = tm // SSM_CHUNK
    L = SSM_CHUNK
    heads_per_group = SSM_HEADS // SSM_GROUPS
    gw = heads_per_group * SSM_HEAD_DIM

    dt = dt_ref[...]
    a = a_ref[...]
    lane_t = lax.broadcasted_iota(jnp.int32, (tm, LANES), 1)
    hi, mid, lo = _split3_f32(dt)
    dt3 = jnp.where(lane_t < SSM_HEADS, hi, jnp.where(lane_t < 2 * SSM_HEADS, mid, lo)).astype(BF16)
    dtb = jnp.dot(dt3, e3_ref[...], preferred_element_type=F32)
    yield

    tril = tril_ref[...]
    li = lax.broadcasted_iota(jnp.int32, (L, L), 0)
    si = lax.broadcasted_iota(jnp.int32, (L, L), 1)
    causal = si <= li
    low_head = lax.broadcasted_iota(jnp.int32, (L, LANES), 1) < SSM_HEAD_DIM
    head_of_lane = lax.broadcasted_iota(jnp.int32, (L, gw), 1) // SSM_HEAD_DIM
    zero_blk = jnp.zeros((SSM_STATE, L), BF16)

    for c in range(nchunk):
        rows = slice(L * c, L * (c + 1))
        hi, mid, lo = _split3_f32(a[rows])
        a_cs = (jnp.dot(tril, hi.astype(BF16), preferred_element_type=F32)
                + jnp.dot(tril, mid.astype(BF16), preferred_element_type=F32)
                + jnp.dot(tril, lo.astype(BF16), preferred_element_type=F32))
        a_cst = a_cs.T
        dtt = dt[rows].T

        for gp in range(SSM_GROUPS // 2):
            g0 = 2 * gp
            ccat = jnp.concatenate([c_ref[g0, rows, :], c_ref[g0 + 1, rows, :]], axis=1)
            bd = jnp.concatenate(
                [jnp.concatenate([bt_ref[g0, c], zero_blk], axis=1),
                 jnp.concatenate([zero_blk, bt_ref[g0 + 1, c]], axis=1)], axis=0)
            cbp = jnp.dot(ccat, bd, preferred_element_type=F32)
            for gi in range(2):
                g = g0 + gi
                cb = cbp[:, L * gi:L * (gi + 1)]
                ms, eas, xds = [], [], []
                for pair in range(heads_per_group // 2):
                    slab = 2 * g + pair
                    acols = []
                    for r in range(2):
                        h = heads_per_group * g + 2 * pair + r
                        acol = jnp.broadcast_to(a_cs[:, h:h + 1], (L, LANES))
                        seg = acol - a_cst[h:h + 1, :]
                        lm = jnp.exp(jnp.where(causal, seg, -jnp.inf))
                        ms.append((cb * lm * dtt[h:h + 1, :]).astype(BF16))
                        acols.append(acol)
                    sel = jnp.where(low_head, acols[0], acols[1])
                    eas.append(jnp.exp(sel))
                    dec = jnp.exp(sel[L - 1:L, :] - sel)
                    xds.append((xs_ref[slab, rows, :] * dtb[rows, LANES * slab:LANES * (slab + 1)] * dec)
                               .astype(BF16))
                ea_g = jnp.concatenate(eas, axis=1)
                cc_ = c_ref[g, rows, :]
                prev = state_ref[g]
                yoff = jnp.dot(cc_, prev.astype(BF16), preferred_element_type=F32) * ea_g
                st_new = jnp.dot(bt_ref[g, c], jnp.concatenate(xds, axis=1), preferred_element_type=F32)
                state_ref[g] = prev * ea_g[L - 1:L, :] + st_new
                xg = jnp.concatenate([xsb_ref[2 * g, rows, :], xsb_ref[2 * g + 1, rows, :]], axis=1)
                zero_x = jnp.zeros_like(xg)
                rhs = jnp.concatenate([jnp.where(head_of_lane == r, xg, zero_x)
                                       for r in range(heads_per_group)], axis=0)
                yd = jnp.dot(jnp.concatenate(ms, axis=1), rhs, preferred_element_type=F32) + yoff
                for pair in range(heads_per_group // 2):
                    slab = 2 * g + pair
                    y_ref[slab, rows, :] = (yd[:, LANES * pair:LANES * (pair + 1)]
                                            + dvec_ref[slab:slab + 1, :] * xs_ref[slab, rows, :])
            yield

    ssq = jnp.zeros((tm, 1), F32)
    zc = 4 * LANES
    for cc in range(d_inner // zc):
        z = jnp.dot(hb_ref[...], wz_ref[:, zc * cc:zc * (cc + 1)], preferred_element_type=F32)
        for jj in range(zc // LANES):
            j = (zc // LANES) * cc + jj
            yg = y_ref[j] * _silu(z[:, LANES * jj:LANES * (jj + 1)])
            y_ref[j] = yg
            ssq = ssq + jnp.sum(yg * yg, axis=-1, keepdims=True)
        yield
    rinv = lax.rsqrt(ssq * (1.0 / d_inner) + SSM_NORM_EPS)
    yn = jnp.concatenate(
        [(y_ref[j] * rinv * gnw_ref[:, LANES * j:LANES * (j + 1)]).astype(BF16) for j in range(n_xs)],
        axis=1)
    o_ref[...] = x_ref[...] + jnp.dot(yn, wout_ref[...], preferred_element_type=F32)


def _ssd_body(xa_ref, xb_ref, nw_ref, win_hbm, wdt_ref, cw_ref, cb_ref, dtb_ref, alog_ref,
              dvec_ref, gnw_ref, wout_hbm, tril_ref, e3_ref, o_ref,
              s_ref, halo_ref, y_ref, state_ref, wz_ref, wxbc_ref, wout_ref,
              w_stage, sem, *bufs, layer, tm, tiles_per_seq):
    i = pl.program_id(0)
    nbuf = len(bufs) // 2
    buf_even, buf_odd = bufs[:nbuf], bufs[nbuf:]

    @pl.when(i == 0)
    def _():
        for r in buf_odd:
            r[...] = jnp.zeros(r.shape, r.dtype)
        d_inner = wz_ref.shape[1]
        conv_dim = wxbc_ref.shape[1]
        rows = w_stage.shape[1]
        _stream_cast(win_hbm.at[layer, pl.ds(0, d_inner), :], wz_ref, w_stage, sem, rows, transpose=True)
        _stream_cast(win_hbm.at[layer, pl.ds(d_inner, conv_dim), :], wxbc_ref, w_stage, sem, rows,
                     transpose=True)
        _stream_cast(wout_hbm.at[layer], wout_ref, w_stage, sem, rows)

    @pl.when(lax.rem(i, tiles_per_seq) == 0)
    def _():
        halo_ref[...] = jnp.zeros(halo_ref.shape, halo_ref.dtype)

    @pl.when((i == 0) | (lax.rem(i - 1, tiles_per_seq) == 0))
    def _():
        state_ref[...] = jnp.zeros(state_ref.shape, state_ref.dtype)

    def both(buf_a, buf_b):
        _round_robin([
            _ssd_stage_a(xa_ref, nw_ref, wxbc_ref, wdt_ref, cw_ref, cb_ref, dtb_ref, alog_ref,
                         s_ref, halo_ref, buf_a, tm=tm),
            _ssd_stage_b(xb_ref, wz_ref, dvec_ref, gnw_ref, wout_ref, tril_ref, e3_ref, o_ref,
                         y_ref, state_ref, buf_b, tm=tm),
        ])

    @pl.when(lax.rem(i, 2) == 0)
    def _():
        both(buf_even, buf_odd)

    @pl.when(lax.rem(i, 2) == 1)
    def _():
        both(buf_odd, buf_even)


def _ssd_layer(xt, nw, w_in_all, conv_w, conv_b, dt_bias, a_log, d_skip, norm_w, w_out_all, *, layer, seq, tm):
    t, d = xt.shape
    d_inner = SSM_HEADS * SSM_HEAD_DIM
    bc_dim = SSM_GROUPS * SSM_STATE
    conv_dim = d_inner + 2 * bc_dim
    n_xs = d_inner // LANES
    n_b = bc_dim // LANES
    nchunk = tm // SSM_CHUNK
    L = SSM_CHUNK
    copies = 3
    pad = LANES - copies * SSM_HEADS

    def lanes3(v):
        return jnp.concatenate([v] * copies + [jnp.zeros(v.shape[:-1] + (pad,), v.dtype)], axis=-1)

    wdt = lanes3(w_in_all[layer][:, d_inner + conv_dim:]).astype(BF16)
    dtb = lanes3(dt_bias.reshape(1, SSM_HEADS))
    alog = lanes3(a_log.reshape(1, SSM_HEADS))
    dvec = jnp.repeat(d_skip, SSM_HEAD_DIM).reshape(n_xs, LANES)
    tril = jnp.asarray(np.tril(np.ones((L, L), np.float32)), BF16)
    rows_h = np.arange(LANES) % SSM_HEADS
    valid = (np.arange(LANES) < copies * SSM_HEADS)[:, None]
    e3 = jnp.asarray(((rows_h[:, None] == (np.arange(d_inner) // SSM_HEAD_DIM)[None, :]) & valid)
                     .astype(np.float32), BF16)
    n_tiles = t // tm
    body = functools.partial(_ssd_body, layer=layer, tm=tm, tiles_per_seq=seq // tm)
    stage_slots, stage_rows = 3, 256
    w_in_t = jnp.swapaxes(w_in_all, 1, 2)
    staged = [
        pltpu.VMEM((tm, d), BF16),
        pltpu.VMEM((n_xs, tm, LANES), F32),
        pltpu.VMEM((n_xs, tm, LANES), BF16),
        pltpu.VMEM((n_b, nchunk, SSM_STATE, L), BF16),
        pltpu.VMEM((n_b, tm, LANES), BF16),
        pltpu.VMEM((tm, LANES), F32),
        pltpu.VMEM((tm, LANES), F32),
    ]
    return pl.pallas_call(
        body,
        out_shape=jax.ShapeDtypeStruct((t, d), F32),
        grid=(n_tiles + 1,),
        in_specs=[
            pl.BlockSpec((tm, d), lambda i: (jnp.minimum(i, n_tiles - 1), 0)),
            pl.BlockSpec((tm, d), lambda i: (jnp.maximum(i - 1, 0), 0)),
            _const_spec((1, d)),
            pl.BlockSpec(memory_space=pl.ANY),
            _const_spec((d, LANES)),
            _const_spec((SSM_CONV, conv_dim)),
            _const_spec((1, conv_dim)),
            _const_spec((1, LANES)),
            _const_spec((1, LANES)),
            _const_spec((n_xs, LANES)),
            _const_spec((1, d_inner)),
            pl.BlockSpec(memory_space=pl.ANY),
            _const_spec((L, L)),
            _const_spec((LANES, d_inner)),
        ],
        out_specs=pl.BlockSpec((tm, d), lambda i: (jnp.maximum(i - 1, 0), 0)),
        scratch_shapes=[
            pltpu.VMEM((2, 4, SUBLANES + tm, LANES), F32),
            pltpu.VMEM((n_xs + 2 * n_b, SUBLANES, LANES), F32),
            pltpu.VMEM((n_xs, tm, LANES), F32),
            pltpu.VMEM((SSM_GROUPS, SSM_STATE, 2 * LANES), F32),
            pltpu.VMEM((d, d_inner), BF16),
            pltpu.VMEM((d, conv_dim), BF16),
            pltpu.VMEM((d_inner, d), BF16),
            pltpu.VMEM((stage_slots, stage_rows, d), F32),
            pltpu.SemaphoreType.DMA((stage_slots,)),
        ] + staged + staged,
        compiler_params=pltpu.CompilerParams(dimension_semantics=("arbitrary",),
                                             vmem_limit_bytes=VMEM_LIMIT_BYTES),
        name="ssd_mixer",
    )(xt, xt, nw.reshape(1, d), w_in_t, wdt, conv_w, conv_b.reshape(1, conv_dim), dtb, alog, dvec,
      norm_w.reshape(1, d_inner), w_out_all, tril, e3)


def _ffn_body(x_ref, nw_ref, wup_hbm, cw_ref, cb_ref, wdn_hbm, fnw_ref, o_ref,
              hb_ref, s_ref, halo_ref, act_ref, wu_ref, wg_ref, wdn_ref,
              up_stage, dn_stage, sem, *, layer, tm, fc, tiles_per_seq, final_norm):
    i = pl.program_id(0)
    d_ff = act_ref.shape[1]
    half_slabs = fc // LANES

    @pl.when(i == 0)
    def _():
        _stream_cast(wup_hbm.at[layer, :, pl.ds(0, d_ff)], wu_ref, up_stage, sem, up_stage.shape[1])
        _stream_cast(wup_hbm.at[layer, :, pl.ds(d_ff, d_ff)], wg_ref, up_stage, sem, up_stage.shape[1])
        _stream_cast(wdn_hbm.at[layer], wdn_ref, dn_stage, sem, dn_stage.shape[1])

    @pl.when(lax.rem(i, tiles_per_seq) == 0)
    def _():
        halo_ref[...] = jnp.zeros(halo_ref.shape, halo_ref.dtype)

    x = x_ref[...]
    hb_ref[...] = _rms(x, nw_ref[...], NORM_EPS).astype(BF16)

    for c in range(d_ff // fc):
        slot = c % 2
        ys = []
        for half, w_ref in enumerate((wu_ref, wg_ref)):
            hid = jnp.dot(hb_ref[...], w_ref[:, fc * c:fc * (c + 1)], preferred_element_type=F32)
            for jj in range(half_slabs):
                j = half * half_slabs + jj
                col = half * d_ff + fc * c + LANES * jj
                gs = col // LANES
                x0 = hid[:, LANES * jj:LANES * (jj + 1)]
                s_ref[slot, j, 0:SUBLANES, :] = halo_ref[gs]
                s_ref[slot, j, SUBLANES:SUBLANES + tm, :] = x0
                halo_ref[gs] = x0[tm - SUBLANES:tm]
                y = x0 * cw_ref[FFN_CONV - 1:FFN_CONV, col:col + LANES] + cb_ref[:, col:col + LANES]
                for k in range(1, FFN_CONV):
                    y = y + (s_ref[slot, j, SUBLANES - k:SUBLANES - k + tm, :]
                             * cw_ref[FFN_CONV - 1 - k:FFN_CONV - k, col:col + LANES])
                ys.append(y)
        for jj in range(half_slabs):
            u = ys[jj]
            g = ys[half_slabs + jj]
            act_ref[:, fc * c + LANES * jj:fc * c + LANES * (jj + 1)] = (_silu(g) * u).astype(BF16)

    out = x + jnp.dot(act_ref[...], wdn_ref[...], preferred_element_type=F32)
    if final_norm:
        out = _rms(out, fnw_ref[...], NORM_EPS)
    o_ref[...] = out


def _ffn_layer(xt, nw, w_up_all, conv_w, conv_b, w_down_all, final_w, *, layer, seq, tm, fc, final_norm):
    t, d = xt.shape
    d_ff = w_down_all.shape[1]
    n_tiles = t // tm
    stage_slots, up_rows, dn_rows = 3, 128, 256
    body = functools.partial(_ffn_body, layer=layer, tm=tm, fc=fc, tiles_per_seq=seq // tm,
                             final_norm=final_norm)
    return pl.pallas_call(
        body,
        out_shape=jax.ShapeDtypeStruct((t, d), F32),
        grid=(n_tiles,),
        in_specs=[
            pl.BlockSpec((tm, d), lambda i: (i, 0)),
            _const_spec((1, d)),
            pl.BlockSpec(memory_space=pl.ANY),
            _const_spec((FFN_CONV, 2 * d_ff)),
            _const_spec((1, 2 * d_ff)),
            pl.BlockSpec(memory_space=pl.ANY),
            _const_spec((1, d)),
        ],
        out_specs=pl.BlockSpec((tm, d), lambda i: (i, 0)),
        scratch_shapes=[
            pltpu.VMEM((tm, d), BF16),
            pltpu.VMEM((2, 2 * fc // LANES, SUBLANES + tm, LANES), F32),
            pltpu.VMEM((2 * d_ff // LANES, SUBLANES, LANES), F32),
            pltpu.VMEM((tm, d_ff), BF16),
            pltpu.VMEM((d, d_ff), BF16),
            pltpu.VMEM((d, d_ff), BF16),
            pltpu.VMEM((d_ff, d), BF16),
            pltpu.VMEM((stage_slots, up_rows, d_ff), F32),
            pltpu.VMEM((stage_slots, dn_rows, d), F32),
            pltpu.SemaphoreType.DMA((stage_slots,)),
        ],
        compiler_params=pltpu.CompilerParams(dimension_semantics=("arbitrary",),
                                             vmem_limit_bytes=VMEM_LIMIT_BYTES),
        name="conv_ffn_final" if final_norm else "conv_ffn",
    )(xt, nw.reshape(1, d), w_up_all, conv_w, conv_b.reshape(1, 2 * d_ff), w_down_all,
      final_w.reshape(1, d))


def kernel(x, positions, norm_mix, norm_ffn, norm_final, mix_w_in, pool_w, pool_scale, attn_sinks,
           mix_w_out, ssm_w_in, ssm_conv_w, ssm_conv_b, ssm_dt_bias, ssm_A_log, ssm_D, ssm_norm,
           ssm_w_out, ffn_w_up, ffn_conv_w, ffn_conv_b, ffn_w_down):
    b, s, d = x.shape
    depth = norm_mix.shape[0]
    xt = x.reshape(b * s, d)
    pos = positions.reshape(b * s)
    for i in range(depth):
        j = i // 2
        if i % 2 == 0:
            xt = _mix0_layer(xt, pos, norm_mix[i], mix_w_in[j], pool_w[j], pool_scale[j], attn_sinks[j],
                             mix_w_out[j], seq=s, tm=512)
        else:
            xt = _ssd_layer(xt, norm_mix[i], ssm_w_in, ssm_conv_w[j], ssm_conv_b[j], ssm_dt_bias[j],
                            ssm_A_log[j], ssm_D[j], ssm_norm[j], ssm_w_out, layer=j, seq=s, tm=256)
        xt = _ffn_layer(xt, norm_ffn[i], ffn_w_up, ffn_conv_w[i], ffn_conv_b[i], ffn_w_down,
                        norm_final, layer=i, seq=s, tm=512, fc=256, final_norm=(i == depth - 1))
    return xt.reshape(b, s, d)
```

```python
import functools
import math

import numpy as np
import jax
import jax.numpy as jnp
from jax import lax
from jax.experimental import pallas as pl
from jax.experimental.pallas import tpu as pltpu

HEAD_DIM = 64
N_HEADS = 8
N_KV_HEADS = 2
GQ = N_HEADS // N_KV_HEADS
ATT_BLOCK = 128
ROPE_THETA = 10000.0
POOL_WINDOWS = (2, 4, 8, 16)
SSM_HEAD_DIM = 64
SSM_HEADS = 32
SSM_GROUPS = 8
SSM_STATE = 128
SSM_CONV = 4
SSM_CHUNK = 128
FFN_CONV = 3
NORM_EPS = 1e-6
SSM_NORM_EPS = 1e-5

LANES = 128
SUBLANES = 8
VMEM_LIMIT_BYTES = 56 * 1024 * 1024
WEIGHT_STAGE_SLOTS = 4
WEIGHT_STAGE_ELEMS = 256 * 1024

MIX_TILE = 512
SSD_TILE = 256
FFN_TILE = 512
FFN_COL_CHUNK = 256

F32 = jnp.float32
BF16 = jnp.bfloat16


def _rms(x, w, eps):
    ms = jnp.mean(x * x, axis=-1, keepdims=True)
    return x * lax.rsqrt(ms + eps) * w


def _silu(x):
    return x * jax.nn.sigmoid(x)


def _const_spec(shape):
    nd = len(shape)
    return pl.BlockSpec(shape, lambda i: (0,) * nd, pipeline_mode=pl.Buffered(1))


def _stream_cast(src, dst_ref, stage_ref, sem, chunk_rows, transpose=False):
    rows = dst_ref.shape[1] if transpose else dst_ref.shape[0]
    n = rows // chunk_rows
    slots = stage_ref.shape[0]

    def copy(k):
        return pltpu.make_async_copy(src.at[pl.ds(k * chunk_rows, chunk_rows), :],
                                     stage_ref.at[k % slots], sem.at[k % slots])

    for k in range(min(slots - 1, n)):
        copy(k).start(priority=k % 2)
    for k in range(n):
        if k + slots - 1 < n:
            copy(k + slots - 1).start(priority=(k + slots - 1) % 2)
        copy(k).wait()
        if transpose:
            dst_ref[:, chunk_rows * k:chunk_rows * (k + 1)] = stage_ref[k % slots].T.astype(BF16)
        else:
            dst_ref[chunk_rows * k:chunk_rows * (k + 1), :] = stage_ref[k % slots].astype(BF16)


def _split3_f32(x):
    hi = x.astype(BF16).astype(F32)
    r1 = x - hi
    mid = r1.astype(BF16).astype(F32)
    lo = (r1 - mid).astype(BF16).astype(F32)
    return hi, mid, lo


MIX_POOL_DIM = LANES * len(POOL_WINDOWS)
MIX_Q_DIM = N_HEADS * HEAD_DIM
MIX_NSLAB = MIX_Q_DIM // LANES


def _mix0_stage_a(x_ref, pos_ref, nw_ref, win_ref, invf_ref, sgn_ref, poolw_ref, pscale_ref,
                  ubuf, buf, tile_in_seq, *, tm):
    qbuf, kt_ref, vt_ref, mixbuf = buf
    nblk = tm // ATT_BLOCK
    halo = max(POOL_WINDOWS)
    pool_dim, q_dim, nslab = MIX_POOL_DIM, MIX_Q_DIM, MIX_NSLAB

    hb = _rms(x_ref[...], nw_ref[...], NORM_EPS).astype(BF16)
    proj = jnp.dot(hb, win_ref[...], preferred_element_type=F32)
    yield

    nfreq = HEAD_DIM // 2
    ngrp = LANES // nfreq
    ang = pos_ref[...].astype(F32) * invf_ref[...]
    grp = lax.broadcasted_iota(jnp.int32, (tm // ngrp, LANES), 1) // nfreq

    def spread(dense):
        rolled = [dense] + [pltpu.roll(dense, nfreq * s, 1) for s in range(1, ngrp)]
        parts = []
        for k in range(ngrp):
            out = rolled[(-k) % ngrp]
            for g in range(1, ngrp):
                out = jnp.where(grp == g, rolled[(g - k) % ngrp], out)
            parts.append(out)
        return jnp.concatenate(parts, axis=0)

    cs = spread(jnp.cos(ang))
    sn = spread(jnp.sin(ang)) * sgn_ref[...]
    lane = lax.broadcasted_iota(jnp.int32, (tm, LANES), 1)
    first_half = (lane & (HEAD_DIM // 2)) == 0
    low_head = lane < HEAD_DIM

    def rope(t):
        up = pltpu.roll(t, LANES - HEAD_DIM // 2, 1)
        dn = pltpu.roll(t, HEAD_DIM // 2, 1)
        return t * cs + jnp.where(first_half, up, dn) * sn

    scale = HEAD_DIM ** -0.5
    for j in range(nslab):
        q = rope(proj[:, pool_dim + LANES * j: pool_dim + LANES * (j + 1)]) * scale
        qa = jnp.where(low_head, q, 0.0).astype(BF16)
        qb = jnp.where(low_head, 0.0, q).astype(BF16)
        for b in range(nblk):
            rows = slice(ATT_BLOCK * b, ATT_BLOCK * (b + 1))
            qbuf[j, b, 0:ATT_BLOCK, :] = qa[rows]
            qbuf[j, b, ATT_BLOCK:2 * ATT_BLOCK, :] = qb[rows]
        yield
    kcol = pool_dim + q_dim
    kr = rope(proj[:, kcol:kcol + LANES])
    for b in range(nblk):
        kt_ref[:, ATT_BLOCK * b:ATT_BLOCK * (b + 1)] = kr[ATT_BLOCK * b:ATT_BLOCK * (b + 1)].T.astype(BF16)
    vt_ref[...] = proj[:, kcol + LANES:kcol + 2 * LANES].astype(BF16)
    yield

    tpos = tile_in_seq * tm + lax.broadcasted_iota(jnp.int32, (tm, 1), 0)
    for g, w in enumerate(POOL_WINDOWS):
        u_g = proj[:, LANES * g:LANES * (g + 1)]
        ubuf[g, halo:halo + tm, :] = u_g
        acc = u_g
        for k in range(1, w):
            acc = acc + ubuf[g, halo - k:halo - k + tm, :]
        cnt = jnp.minimum(tpos + 1, w).astype(F32)
        pooled = acc / cnt - u_g
        pm = jnp.dot(pooled.astype(BF16), poolw_ref[g], preferred_element_type=F32)
        mixbuf[:, LANES * g:LANES * (g + 1)] = (pm * pscale_ref[:, LANES * g:LANES * (g + 1)]).astype(BF16)
        ubuf[g, 0:halo, :] = ubuf[g, tm:tm + halo, :]
        yield


def _mix0_stage_b(x_ref, sink_ref, wout_ref, o_ref, kcarry, vcarry, buf, tile_in_seq, *, tm):
    qbuf, kt_ref, vt_ref, mixbuf = buf
    nblk = tm // ATT_BLOCK
    pool_dim, nslab = MIX_POOL_DIM, MIX_NSLAB

    qi = lax.broadcasted_iota(jnp.int32, (2 * ATT_BLOCK, 2 * ATT_BLOCK), 0) & (ATT_BLOCK - 1)
    kj = lax.broadcasted_iota(jnp.int32, (2 * ATT_BLOCK, 2 * ATT_BLOCK), 1)
    rel = qi + ATT_BLOCK - kj
    band = (rel >= 0) & (rel < ATT_BLOCK)
    first_lim = jnp.where(tile_in_seq == 0, ATT_BLOCK, 0)
    row2 = lax.broadcasted_iota(jnp.int32, (2 * ATT_BLOCK, 1), 0)
    lane_o = lax.broadcasted_iota(jnp.int32, (ATT_BLOCK, LANES), 1)
    for b in range(nblk):
        if b == 0:
            kk = jnp.concatenate([kcarry[...], kt_ref[:, 0:ATT_BLOCK]], axis=1)
            vv = jnp.concatenate([vcarry[...], vt_ref[0:ATT_BLOCK, :]], axis=0)
        else:
            kk = kt_ref[:, ATT_BLOCK * (b - 1):ATT_BLOCK * (b + 1)]
            vv = vt_ref[ATT_BLOCK * (b - 1):ATT_BLOCK * (b + 1), :]
        mask = (band & (kj >= first_lim)) if b == 0 else band
        for j in range(nslab):
            s = jnp.dot(qbuf[j, b], kk, preferred_element_type=F32)
            s = jnp.where(mask, s, -jnp.inf)
            sink = jnp.where(row2 < ATT_BLOCK, sink_ref[j], sink_ref[j + GQ])
            m = jnp.maximum(jnp.max(s, axis=-1, keepdims=True), sink)
            e = jnp.exp(s - m)
            den = jnp.sum(e, axis=-1, keepdims=True) + jnp.exp(sink - m)
            pv = jnp.dot(e.astype(BF16), vv, preferred_element_type=F32) / den
            o = jnp.where(lane_o < HEAD_DIM, pv[0:ATT_BLOCK], pv[ATT_BLOCK:2 * ATT_BLOCK])
            mixbuf[ATT_BLOCK * b:ATT_BLOCK * (b + 1),
                   pool_dim + LANES * j:pool_dim + LANES * (j + 1)] = o.astype(BF16)
            yield
    kcarry[...] = kt_ref[:, tm - ATT_BLOCK:tm]
    vcarry[...] = vt_ref[tm - ATT_BLOCK:tm, :]

    o_ref[...] = x_ref[...] + jnp.dot(mixbuf[...], wout_ref[...], preferred_element_type=F32)


def _round_robin(stages):
    stages = list(stages)
    while stages:
        for st in list(stages):
            if next(st, StopIteration) is StopIteration:
                stages.remove(st)


def _mix0_body(xa_ref, xb_ref, pos_ref, nw_ref, win_ref, invf_ref, sgn_ref, poolw_ref, pscale_ref,
               sink_ref, wout_ref, o_ref, ubuf, kcarry, vcarry, *bufs, tm, tiles_per_seq):
    i = pl.program_id(0)
    nbuf = len(bufs) // 2
    buf_even, buf_odd = bufs[:nbuf], bufs[nbuf:]
    halo = max(POOL_WINDOWS)
    tile_a = lax.rem(i, tiles_per_seq)
    tile_b = lax.rem(i - 1, tiles_per_seq)

    @pl.when(i == 0)
    def _():
        for r in buf_odd + (kcarry, vcarry):
            r[...] = jnp.zeros(r.shape, r.dtype)

    @pl.when(tile_a == 0)
    def _():
        ubuf[:, 0:halo, :] = jnp.zeros((len(POOL_WINDOWS), halo, LANES), F32)

    def both(buf_a, buf_b):
        _round_robin([
            _mix0_stage_a(xa_ref, pos_ref, nw_ref, win_ref, invf_ref, sgn_ref, poolw_ref, pscale_ref,
                          ubuf, buf_a, tile_a, tm=tm),
            _mix0_stage_b(xb_ref, sink_ref, wout_ref, o_ref, kcarry, vcarry, buf_b, tile_b, tm=tm),
        ])

    @pl.when(lax.rem(i, 2) == 0)
    def _():
        both(buf_even, buf_odd)

    @pl.when(lax.rem(i, 2) == 1)
    def _():
        both(buf_odd, buf_even)


def _mix0_layer(xt, pos, nw, w_in, pool_w, pool_scale, sinks, w_out, *, seq, tm):
    t, d = xt.shape
    pool_dim = LANES * len(POOL_WINDOWS)
    q_dim = N_HEADS * HEAD_DIM
    nslab = q_dim // LANES
    wq = w_in[:, pool_dim:pool_dim + q_dim].reshape(d, N_KV_HEADS, GQ, HEAD_DIM)
    wq = jnp.transpose(wq, (0, 2, 1, 3)).reshape(d, q_dim)
    w_in_p = jnp.concatenate([w_in[:, :pool_dim], wq, w_in[:, pool_dim + q_dim:]], axis=1).astype(BF16)
    wo = w_out[pool_dim:].reshape(N_KV_HEADS, GQ, HEAD_DIM, d)
    wo = jnp.transpose(wo, (1, 0, 2, 3)).reshape(q_dim, d)
    w_out_p = jnp.concatenate([w_out[:pool_dim], wo], axis=0).astype(BF16)
    nfreq = HEAD_DIM // 2
    ngrp = LANES // nfreq
    inv_freq = ROPE_THETA ** (-jnp.arange(0, HEAD_DIM, 2, dtype=F32) / HEAD_DIM)
    invf = jnp.tile(inv_freq, ngrp)[None, :]
    sgn = jnp.asarray(np.where((np.arange(LANES) % HEAD_DIM) < nfreq, -1.0, 1.0)[None, :], F32)
    pos_d = jnp.transpose(pos.reshape(t // tm, ngrp, tm // ngrp), (0, 2, 1))
    pos_d = jnp.repeat(pos_d, nfreq, axis=2).reshape(t // ngrp, LANES)
    mix_in = w_in_p.shape[1]
    nblk = tm // ATT_BLOCK
    halo = max(POOL_WINDOWS)
    n_tiles = t // tm
    body = functools.partial(_mix0_body, tm=tm, tiles_per_seq=seq // tm)
    staged = [
        pltpu.VMEM((nslab, nblk, 2 * ATT_BLOCK, LANES), BF16),
        pltpu.VMEM((LANES, tm), BF16),
        pltpu.VMEM((tm, LANES), BF16),
        pltpu.VMEM((tm, pool_dim + q_dim), BF16),
    ]
    return pl.pallas_call(
        body,
        out_shape=jax.ShapeDtypeStruct((t, d), F32),
        grid=(n_tiles + 1,),
        in_specs=[
            pl.BlockSpec((tm, d), lambda i: (jnp.minimum(i, n_tiles - 1), 0)),
            pl.BlockSpec((tm, d), lambda i: (jnp.maximum(i - 1, 0), 0)),
            pl.BlockSpec((tm // ngrp, LANES), lambda i: (jnp.minimum(i, n_tiles - 1), 0)),
            _const_spec((1, d)),
            _const_spec((d, mix_in)),
            _const_spec((1, LANES)),
            _const_spec((1, LANES)),
            _const_spec(pool_w.shape),
            _const_spec((1, pool_dim)),
            pl.BlockSpec(memory_space=pltpu.SMEM),
            _const_spec((pool_dim + q_dim, d)),
        ],
        out_specs=pl.BlockSpec((tm, d), lambda i: (jnp.maximum(i - 1, 0), 0)),
        scratch_shapes=[
            pltpu.VMEM((len(POOL_WINDOWS), halo + tm, LANES), F32),
            pltpu.VMEM((ATT_BLOCK, LANES), BF16),
            pltpu.VMEM((ATT_BLOCK, LANES), BF16),
        ] + staged + staged,
        compiler_params=pltpu.CompilerParams(dimension_semantics=("arbitrary",),
                                             vmem_limit_bytes=VMEM_LIMIT_BYTES),
        name="pool_swa_mixer",
    )(xt, xt, pos_d, nw.reshape(1, d), w_in_p, invf, sgn, pool_w.astype(BF16),
      pool_scale.reshape(1, pool_dim), sinks, w_out_p)


SSD_D_INNER = SSM_HEADS * SSM_HEAD_DIM
SSD_BC_DIM = SSM_GROUPS * SSM_STATE
SSD_N_XS = SSD_D_INNER // LANES
SSD_N_B = SSD_BC_DIM // LANES


def _ssd_stage_a(x_ref, nw_ref, wxbc_ref, wdt_ref, cw_ref, cb_ref, dtb_ref, alog_ref,
                 s_ref, halo_ref, buf, *, tm):
    hb_ref, xs_ref, xsb_ref, bt_ref, c_ref, dt_ref, a_ref = buf
    n_xs, n_b = SSD_N_XS, SSD_N_B
    n_conv_slabs = n_xs + 2 * n_b
    cw_cols = 4 * LANES
    slabs_per_chunk = cw_cols // LANES
    nchunk = tm // SSM_CHUNK
    L = SSM_CHUNK

    hb_ref[...] = _rms(x_ref[...], nw_ref[...], NORM_EPS).astype(BF16)
    yield

    n_cchunks = n_conv_slabs // slabs_per_chunk
    first_bc = n_xs // slabs_per_chunk
    for step, cc in enumerate(list(range(first_bc, n_cchunks)) + list(range(first_bc))):
        slot = step % 2
        hid = jnp.dot(hb_ref[...], wxbc_ref[:, cw_cols * cc:cw_cols * (cc + 1)],
                      preferred_element_type=F32)
        for jj in range(slabs_per_chunk):
            j = slabs_per_chunk * cc + jj
            cols = slice(LANES * j, LANES * (j + 1))
            x0 = hid[:, LANES * jj:LANES * (jj + 1)]
            s_ref[slot, jj, 0:SUBLANES, :] = halo_ref[j]
            s_ref[slot, jj, SUBLANES:SUBLANES + tm, :] = x0
            halo_ref[j] = x0[tm - SUBLANES:tm]
            y = x0 * cw_ref[SSM_CONV - 1:SSM_CONV, cols] + cb_ref[:, cols]
            for k in range(1, SSM_CONV):
                y = y + (s_ref[slot, jj, SUBLANES - k:SUBLANES - k + tm, :]
                         * cw_ref[SSM_CONV - 1 - k:SSM_CONV - k, cols])
            y = _silu(y)
            if j < n_xs:
                xs_ref[j] = y
                xsb_ref[j] = y.astype(BF16)
            elif j < n_xs + n_b:
                for c in range(nchunk):
                    bt_ref[j - n_xs, c] = y[L * c:L * (c + 1)].T.astype(BF16)
            else:
                c_ref[j - n_xs - n_b] = y.astype(BF16)
        yield

    dt = jax.nn.softplus(jnp.dot(hb_ref[...], wdt_ref[...], preferred_element_type=F32) + dtb_ref[...])
    dt_ref[...] = dt
    a_ref[...] = dt * (-jnp.exp(alog_ref[...]))


def _ssd_stage_b(x_ref, wz_ref, dvec_ref, gnw_ref, wout_ref, tril_ref, e3_ref, o_ref,
                 y_ref, state_ref, buf, *, tm):
    hb_ref, xs_ref, xsb_ref, bt_ref, c_ref, dt_ref, a_ref = buf
    d_inner, n_xs = SSD_D_INNER, SSD_N_XS
    nchunk = tm // SSM_CHUNK
    L = SSM_CHUNK
    heads_per_group = SSM_HEADS // SSM_GROUPS
    gw = heads_per_group * SSM_HEAD_DIM

    dt = dt_ref[...]
    a = a_ref[...]
    lane_t = lax.broadcasted_iota(jnp.int32, (tm, LANES), 1)
    hi, mid, lo = _split3_f32(dt)
    dt3 = jnp.where(lane_t < SSM_HEADS, hi, jnp.where(lane_t < 2 * SSM_HEADS, mid, lo)).astype(BF16)
    dtb = jnp.dot(dt3, e3_ref[...], preferred_element_type=F32)
    yield

    tril = tril_ref[...]
    li = lax.broadcasted_iota(jnp.int32, (L, L), 0)
    si = lax.broadcasted_iota(jnp.int32, (L, L), 1)
    causal = si <= li
    low_head = lax.broadcasted_iota(jnp.int32, (L, LANES), 1) < SSM_HEAD_DIM
    head_of_lane = lax.broadcasted_iota(jnp.int32, (L, gw), 1) // SSM_HEAD_DIM
    zero_blk = jnp.zeros((SSM_STATE, L), BF16)

    for c in range(nchunk):
        rows = slice(L * c, L * (c + 1))
        hi, mid, lo = _split3_f32(a[rows])
        a_cs = (jnp.dot(tril, hi.astype(BF16), preferred_element_type=F32)
                + jnp.dot(tril, mid.astype(BF16), preferred_element_type=F32)
                + jnp.dot(tril, lo.astype(BF16), preferred_element_type=F32))
        a_cst = a_cs.T
        dtt = dt[rows].T

        for gp in range(SSM_GROUPS // 2):
            g0 = 2 * gp
            ccat = jnp.concatenate([c_ref[g0, rows, :], c_ref[g0 + 1, rows, :]], axis=1)
            bd = jnp.concatenate(
                [jnp.concatenate([bt_ref[g0, c], zero_blk], axis=1),
                 jnp.concatenate([zero_blk, bt_ref[g0 + 1, c]], axis=1)], axis=0)
            cbp = jnp.dot(ccat, bd, preferred_element_type=F32)
            for gi in range(2):
                g = g0 + gi
                cb = cbp[:, L * gi:L * (gi + 1)]
                ms, eas, xds = [], [], []
                for pair in range(heads_per_group // 2):
                    slab = 2 * g + pair
                    acols = []
                    for r in range(2):
                        h = heads_per_group * g + 2 * pair + r
                        acol = jnp.broadcast_to(a_cs[:, h:h + 1], (L, LANES))
                        seg = acol - a_cst[h:h + 1, :]
                        lm = jnp.exp(jnp.where(causal, seg, -jnp.inf))
                        ms.append((cb * lm * dtt[h:h + 1, :]).astype(BF16))
                        acols.append(acol)
                    sel = jnp.where(low_head, acols[0], acols[1])
                    eas.append(jnp.exp(sel))
                    dec = jnp.exp(sel[L - 1:L, :] - sel)
                    xds.append((xs_ref[slab, rows, :] * dtb[rows, LANES * slab:LANES * (slab + 1)] * dec)
                               .astype(BF16))
                ea_g = jnp.concatenate(eas, axis=1)
                cc_ = c_ref[g, rows, :]
                prev = state_ref[g]
                yoff = jnp.dot(cc_, prev.astype(BF16), preferred_element_type=F32) * ea_g
                st_new = jnp.dot(bt_ref[g, c], jnp.concatenate(xds, axis=1), preferred_element_type=F32)
                state_ref[g] = prev * ea_g[L - 1:L, :] + st_new
                xg = jnp.concatenate([xsb_ref[2 * g, rows, :], xsb_ref[2 * g + 1, rows, :]], axis=1)
                zero_x = jnp.zeros_like(xg)
                rhs = jnp.concatenate([jnp.where(head_of_lane == r, xg, zero_x)
                                       for r in range(heads_per_group)], axis=0)
                yd = jnp.dot(jnp.concatenate(ms, axis=1), rhs, preferred_element_type=F32) + yoff
                for pair in range(heads_per_group // 2):
                    slab = 2 * g + pair
                    y_ref[slab, rows, :] = (yd[:, LANES * pair:LANES * (pair + 1)]
                                            + dvec_ref[slab:slab + 1, :] * xs_ref[slab, rows, :])
            yield

    ssq = jnp.zeros((tm, 1), F32)
    zc = 4 * LANES
    for cc in range(d_inner // zc):
        z = jnp.dot(hb_ref[...], wz_ref[:, zc * cc:zc * (cc + 1)], preferred_element_type=F32)
        for jj in range(zc // LANES):
            j = (zc // LANES) * cc + jj
            yg = y_ref[j] * _silu(z[:, LANES * jj:LANES * (jj + 1)])
            y_ref[j] = yg
            ssq = ssq + jnp.sum(yg * yg, axis=-1, keepdims=True)
        yield
    rinv = lax.rsqrt(ssq * (1.0 / d_inner) + SSM_NORM_EPS)
    yn = jnp.concatenate(
        [(y_ref[j] * rinv * gnw_ref[:, LANES * j:LANES * (j + 1)]).astype(BF16) for j in range(n_xs)],
        axis=1)
    o_ref[...] = x_ref[...] + jnp.dot(yn, wout_ref[...], preferred_element_type=F32)


def _ssd_body(xa_ref, xb_ref, nw_ref, win_hbm, cw_ref, cb_ref, dtb_ref, alog_ref,
              dvec_ref, gnw_ref, wout_hbm, tril_ref, e3_ref, o_ref,
              s_ref, halo_ref, y_ref, state_ref, wz_ref, wxbc_ref, wdt_ref, wout_ref,
              w_stage, sem, *bufs, layer, tm, tiles_per_seq):
    i = pl.program_id(0)
    nbuf = len(bufs) // 2
    buf_even, buf_odd = bufs[:nbuf], bufs[nbuf:]

    @pl.when(i == 0)
    def _():
        for r in buf_odd:
            r[...] = jnp.zeros(r.shape, r.dtype)
        d_inner = wz_ref.shape[1]
        conv_dim = wxbc_ref.shape[1]
        rows = w_stage.shape[1]
        _stream_cast(win_hbm.at[layer, pl.ds(0, d_inner), :], wz_ref, w_stage, sem, rows, transpose=True)
        _stream_cast(win_hbm.at[layer, pl.ds(d_inner, conv_dim), :], wxbc_ref, w_stage, sem, rows,
                     transpose=True)
        _stream_cast(wout_hbm.at[layer], wout_ref, w_stage, sem, rows)
        dt_copy = pltpu.make_async_copy(win_hbm.at[layer, pl.ds(d_inner + conv_dim, SSM_HEADS), :],
                                        w_stage.at[0, pl.ds(0, SSM_HEADS), :], sem.at[0])
        dt_copy.start()
        dt_copy.wait()
        wd = w_stage[0, 0:SSM_HEADS, :]
        copies = LANES // SSM_HEADS - 1
        wdt_ref[...] = jnp.concatenate([wd] * copies + [jnp.zeros_like(wd)], axis=0).T.astype(BF16)

    @pl.when(lax.rem(i, tiles_per_seq) == 0)
    def _():
        halo_ref[...] = jnp.zeros(halo_ref.shape, halo_ref.dtype)

    @pl.when((i == 0) | (lax.rem(i - 1, tiles_per_seq) == 0))
    def _():
        state_ref[...] = jnp.zeros(state_ref.shape, state_ref.dtype)

    def both(buf_a, buf_b):
        _round_robin([
            _ssd_stage_a(xa_ref, nw_ref, wxbc_ref, wdt_ref, cw_ref, cb_ref, dtb_ref, alog_ref,
                         s_ref, halo_ref, buf_a, tm=tm),
            _ssd_stage_b(xb_ref, wz_ref, dvec_ref, gnw_ref, wout_ref, tril_ref, e3_ref, o_ref,
                         y_ref, state_ref, buf_b, tm=tm),
        ])

    @pl.when(lax.rem(i, 2) == 0)
    def _():
        both(buf_even, buf_odd)

    @pl.when(lax.rem(i, 2) == 1)
    def _():
        both(buf_odd, buf_even)


def _ssd_layer(xt, nw, w_in_all, conv_w, conv_b, dt_bias, a_log, d_skip, norm_w, w_out_all, *, layer, seq, tm):
    t, d = xt.shape
    d_inner = SSM_HEADS * SSM_HEAD_DIM
    bc_dim = SSM_GROUPS * SSM_STATE
    conv_dim = d_inner + 2 * bc_dim
    n_xs = d_inner // LANES
    n_b = bc_dim // LANES
    nchunk = tm // SSM_CHUNK
    L = SSM_CHUNK
    copies = 3
    pad = LANES - copies * SSM_HEADS

    def lanes3(v):
        return jnp.concatenate([v] * copies + [jnp.zeros(v.shape[:-1] + (pad,), v.dtype)], axis=-1)

    w_in_t = jnp.swapaxes(w_in_all, 1, 2)
    dtb = lanes3(dt_bias.reshape(1, SSM_HEADS))
    alog = lanes3(a_log.reshape(1, SSM_HEADS))
    dvec = jnp.repeat(d_skip, SSM_HEAD_DIM).reshape(n_xs, LANES)
    tril = jnp.asarray(np.tril(np.ones((L, L), np.float32)), BF16)
    rows_h = np.arange(LANES) % SSM_HEADS
    valid = (np.arange(LANES) < copies * SSM_HEADS)[:, None]
    e3 = jnp.asarray(((rows_h[:, None] == (np.arange(d_inner) // SSM_HEAD_DIM)[None, :]) & valid)
                     .astype(np.float32), BF16)
    n_tiles = t // tm
    body = functools.partial(_ssd_body, layer=layer, tm=tm, tiles_per_seq=seq // tm)
    stage_slots, stage_rows = WEIGHT_STAGE_SLOTS, WEIGHT_STAGE_ELEMS // d
    staged = [
        pltpu.VMEM((tm, d), BF16),
        pltpu.VMEM((n_xs, tm, LANES), F32),
        pltpu.VMEM((n_xs, tm, LANES), BF16),
        pltpu.VMEM((n_b, nchunk, SSM_STATE, L), BF16),
        pltpu.VMEM((n_b, tm, LANES), BF16),
        pltpu.VMEM((tm, LANES), F32),
        pltpu.VMEM((tm, LANES), F32),
    ]
    return pl.pallas_call(
        body,
        out_shape=jax.ShapeDtypeStruct((t, d), F32),
        grid=(n_tiles + 1,),
        in_specs=[
            pl.BlockSpec((tm, d), lambda i: (jnp.minimum(i, n_tiles - 1), 0)),
            pl.BlockSpec((tm, d), lambda i: (jnp.maximum(i - 1, 0), 0)),
            _const_spec((1, d)),
            pl.BlockSpec(memory_space=pl.ANY),
            _const_spec((SSM_CONV, conv_dim)),
            _const_spec((1, conv_dim)),
            _const_spec((1, LANES)),
            _const_spec((1, LANES)),
            _const_spec((n_xs, LANES)),
            _const_spec((1, d_inner)),
            pl.BlockSpec(memory_space=pl.ANY),
            _const_spec((L, L)),
            _const_spec((LANES, d_inner)),
        ],
        out_specs=pl.BlockSpec((tm, d), lambda i: (jnp.maximum(i - 1, 0), 0)),
        scratch_shapes=[
            pltpu.VMEM((2, 4, SUBLANES + tm, LANES), F32),
            pltpu.VMEM((n_xs + 2 * n_b, SUBLANES, LANES), F32),
            pltpu.VMEM((n_xs, tm, LANES), F32),
            pltpu.VMEM((SSM_GROUPS, SSM_STATE, 2 * LANES), F32),
            pltpu.VMEM((d, d_inner), BF16),
            pltpu.VMEM((d, conv_dim), BF16),
            pltpu.VMEM((d, LANES), BF16),
            pltpu.VMEM((d_inner, d), BF16),
            pltpu.VMEM((stage_slots, stage_rows, d), F32),
            pltpu.SemaphoreType.DMA((stage_slots,)),
        ] + staged + staged,
        compiler_params=pltpu.CompilerParams(dimension_semantics=("arbitrary",),
                                             vmem_limit_bytes=VMEM_LIMIT_BYTES),
        name="ssd_mixer",
    )(xt, xt, nw.reshape(1, d), w_in_t, conv_w, conv_b.reshape(1, conv_dim), dtb, alog, dvec,
      norm_w.reshape(1, d_inner), w_out_all, tril, e3)


def _ffn_body(x_ref, nw_ref, wup_hbm, cw_ref, cb_ref, wdn_hbm, fnw_ref, o_ref,
              hb_ref, s_ref, halo_ref, act_ref, wu_ref, wg_ref, wdn_ref,
              up_stage, dn_stage, sem, *, layer, tm, fc, tiles_per_seq, final_norm):
    i = pl.program_id(0)
    d_ff = act_ref.shape[1]
    half_slabs = fc // LANES

    @pl.when(i == 0)
    def _():
        _stream_cast(wup_hbm.at[layer, :, pl.ds(0, d_ff)], wu_ref, up_stage, sem, up_stage.shape[1])
        _stream_cast(wup_hbm.at[layer, :, pl.ds(d_ff, d_ff)], wg_ref, up_stage, sem, up_stage.shape[1])
        _stream_cast(wdn_hbm.at[layer], wdn_ref, dn_stage, sem, dn_stage.shape[1])

    @pl.when(lax.rem(i, tiles_per_seq) == 0)
    def _():
        halo_ref[...] = jnp.zeros(halo_ref.shape, halo_ref.dtype)

    x = x_ref[...]
    hb_ref[...] = _rms(x, nw_ref[...], NORM_EPS).astype(BF16)

    for c in range(d_ff // fc):
        slot = c % 2
        ys = []
        for half, w_ref in enumerate((wu_ref, wg_ref)):
            hid = jnp.dot(hb_ref[...], w_ref[:, fc * c:fc * (c + 1)], preferred_element_type=F32)
            for jj in range(half_slabs):
                j = half * half_slabs + jj
                col = half * d_ff + fc * c + LANES * jj
                gs = col // LANES
                x0 = hid[:, LANES * jj:LANES * (jj + 1)]
                s_ref[slot, j, 0:SUBLANES, :] = halo_ref[gs]
                s_ref[slot, j, SUBLANES:SUBLANES + tm, :] = x0
                halo_ref[gs] = x0[tm - SUBLANES:tm]
                y = x0 * cw_ref[FFN_CONV - 1:FFN_CONV, col:col + LANES] + cb_ref[:, col:col + LANES]
                for k in range(1, FFN_CONV):
                    y = y + (s_ref[slot, j, SUBLANES - k:SUBLANES - k + tm, :]
                             * cw_ref[FFN_CONV - 1 - k:FFN_CONV - k, col:col + LANES])
                ys.append(y)
        for jj in range(half_slabs):
            u = ys[jj]
            g = ys[half_slabs + jj]
            act_ref[:, fc * c + LANES * jj:fc * c + LANES * (jj + 1)] = (_silu(g) * u).astype(BF16)

    out = x + jnp.dot(act_ref[...], wdn_ref[...], preferred_element_type=F32)
    if final_norm:
        out = _rms(out, fnw_ref[...], NORM_EPS)
    o_ref[...] = out


def _ffn_layer(xt, nw, w_up_all, conv_w, conv_b, w_down_all, final_w, *, layer, seq, tm, fc, final_norm):
    t, d = xt.shape
    d_ff = w_down_all.shape[1]
    n_tiles = t // tm
    stage_slots = WEIGHT_STAGE_SLOTS
    up_rows = LANES
    dn_rows = WEIGHT_STAGE_ELEMS // d
    assert d % up_rows == 0 and d_ff % dn_rows == 0
    body = functools.partial(_ffn_body, layer=layer, tm=tm, fc=fc, tiles_per_seq=seq // tm,
                             final_norm=final_norm)
    return pl.pallas_call(
        body,
        out_shape=jax.ShapeDtypeStruct((t, d), F32),
        grid=(n_tiles,),
        in_specs=[
            pl.BlockSpec((tm, d), lambda i: (i, 0)),
            _const_spec((1, d)),
            pl.BlockSpec(memory_space=pl.ANY),
            _const_spec((FFN_CONV, 2 * d_ff)),
            _const_spec((1, 2 * d_ff)),
            pl.BlockSpec(memory_space=pl.ANY),
            _const_spec((1, d)),
        ],
        out_specs=pl.BlockSpec((tm, d), lambda i: (i, 0)),
        scratch_shapes=[
            pltpu.VMEM((tm, d), BF16),
            pltpu.VMEM((2, 2 * fc // LANES, SUBLANES + tm, LANES), F32),
            pltpu.VMEM((2 * d_ff // LANES, SUBLANES, LANES), F32),
            pltpu.VMEM((tm, d_ff), BF16),
            pltpu.VMEM((d, d_ff), BF16),
            pltpu.VMEM((d, d_ff), BF16),
            pltpu.VMEM((d_ff, d), BF16),
            pltpu.VMEM((stage_slots, up_rows, d_ff), F32),
            pltpu.VMEM((stage_slots, dn_rows, d), F32),
            pltpu.SemaphoreType.DMA((stage_slots,)),
        ],
        compiler_params=pltpu.CompilerParams(dimension_semantics=("arbitrary",),
                                             vmem_limit_bytes=VMEM_LIMIT_BYTES),
        name="conv_ffn_final" if final_norm else "conv_ffn",
    )(xt, nw.reshape(1, d), w_up_all, conv_w, conv_b.reshape(1, 2 * d_ff), w_down_all,
      final_w.reshape(1, d))


def kernel(x, positions, norm_mix, norm_ffn, norm_final, mix_w_in, pool_w, pool_scale, attn_sinks,
           mix_w_out, ssm_w_in, ssm_conv_w, ssm_conv_b, ssm_dt_bias, ssm_A_log, ssm_D, ssm_norm,
           ssm_w_out, ffn_w_up, ffn_conv_w, ffn_conv_b, ffn_w_down):
    b, s, d = x.shape
    depth = norm_mix.shape[0]
    xt = x.reshape(b * s, d)
    pos = positions.reshape(b * s)
    for i in range(depth):
        j = i // 2
        if i % 2 == 0:
            xt = _mix0_layer(xt, pos, norm_mix[i], mix_w_in[j], pool_w[j], pool_scale[j], attn_sinks[j],
                             mix_w_out[j], seq=s, tm=MIX_TILE)
        else:
            xt = _ssd_layer(xt, norm_mix[i], ssm_w_in, ssm_conv_w[j], ssm_conv_b[j], ssm_dt_bias[j],
                            ssm_A_log[j], ssm_D[j], ssm_norm[j], ssm_w_out, layer=j, seq=s, tm=SSD_TILE)
        xt = _ffn_layer(xt, norm_ffn[i], ffn_w_up, ffn_conv_w[i], ffn_conv_b[i], ffn_w_down,
                        norm_final, layer=i, seq=s, tm=FFN_TILE, fc=FFN_COL_CHUNK,
                        final_norm=(i == depth - 1))
    return xt.reshape(b, s, d)
```

```python
import functools
import math

import numpy as np
import jax
import jax.numpy as jnp
from jax import lax
from jax.experimental import pallas as pl
from jax.experimental.pallas import tpu as pltpu

HEAD_DIM = 64
N_HEADS = 8
N_KV_HEADS = 2
GQ = N_HEADS // N_KV_HEADS
ATT_BLOCK = 128
ROPE_THETA = 10000.0
POOL_WINDOWS = (2, 4, 8, 16)
SSM_HEAD_DIM = 64
SSM_HEADS = 32
SSM_GROUPS = 8
SSM_STATE = 128
SSM_CONV = 4
SSM_CHUNK = 128
FFN_CONV = 3
NORM_EPS = 1e-6
SSM_NORM_EPS = 1e-5

LANES = 128
SUBLANES = 8
VMEM_LIMIT_BYTES = 56 * 1024 * 1024
WEIGHT_STAGE_SLOTS = 4
WEIGHT_STAGE_ELEMS = 256 * 1024

MIX_TILE = 512
SSD_TILE = 256
FFN_TILE = 512
FFN_COL_CHUNK = 256

F32 = jnp.float32
BF16 = jnp.bfloat16


def _rms(x, w, eps):
    ms = jnp.mean(x * x, axis=-1, keepdims=True)
    return x * lax.rsqrt(ms + eps) * w


def _silu(x):
    return x * jax.nn.sigmoid(x)


def _const_spec(shape):
    nd = len(shape)
    return pl.BlockSpec(shape, lambda i: (0,) * nd, pipeline_mode=pl.Buffered(1))


def _stream_cast(src, dst_ref, stage_ref, sem, chunk_rows, transpose=False):
    rows = dst_ref.shape[1] if transpose else dst_ref.shape[0]
    n = rows // chunk_rows
    slots = stage_ref.shape[0]

    def copy(k):
        return pltpu.make_async_copy(src.at[pl.ds(k * chunk_rows, chunk_rows), :],
                                     stage_ref.at[k % slots], sem.at[k % slots])

    for k in range(min(slots - 1, n)):
        copy(k).start(priority=k % 2)
    for k in range(n):
        if k + slots - 1 < n:
            copy(k + slots - 1).start(priority=(k + slots - 1) % 2)
        copy(k).wait()
        if transpose:
            dst_ref[:, chunk_rows * k:chunk_rows * (k + 1)] = stage_ref[k % slots].T.astype(BF16)
        else:
            dst_ref[chunk_rows * k:chunk_rows * (k + 1), :] = stage_ref[k % slots].astype(BF16)


def _split3_f32(x):
    hi = x.astype(BF16).astype(F32)
    r1 = x - hi
    mid = r1.astype(BF16).astype(F32)
    lo = (r1 - mid).astype(BF16).astype(F32)
    return hi, mid, lo


MIX_POOL_DIM = LANES * len(POOL_WINDOWS)
MIX_Q_DIM = N_HEADS * HEAD_DIM
MIX_NSLAB = MIX_Q_DIM // LANES


def _mix0_stage_a(x_ref, pos_ref, nw_ref, win_ref, invf_ref, sgn_ref, poolw_ref, pscale_ref,
                  ubuf, buf, tile_in_seq, *, tm):
    qbuf, kt_ref, vt_ref, mixbuf = buf
    nblk = tm // ATT_BLOCK
    halo = max(POOL_WINDOWS)
    pool_dim, q_dim, nslab = MIX_POOL_DIM, MIX_Q_DIM, MIX_NSLAB

    hb = _rms(x_ref[...], nw_ref[...], NORM_EPS).astype(BF16)
    proj = jnp.dot(hb, win_ref[...], preferred_element_type=F32)
    yield

    nfreq = HEAD_DIM // 2
    ngrp = LANES // nfreq
    ang = pos_ref[...].astype(F32) * invf_ref[...]
    grp = lax.broadcasted_iota(jnp.int32, (tm // ngrp, LANES), 1) // nfreq

    def spread(dense):
        rolled = [dense] + [pltpu.roll(dense, nfreq * s, 1) for s in range(1, ngrp)]
        parts = []
        for k in range(ngrp):
            out = rolled[(-k) % ngrp]
            for g in range(1, ngrp):
                out = jnp.where(grp == g, rolled[(g - k) % ngrp], out)
            parts.append(out)
        return jnp.concatenate(parts, axis=0)

    cs = spread(jnp.cos(ang))
    sn = spread(jnp.sin(ang)) * sgn_ref[...]
    lane = lax.broadcasted_iota(jnp.int32, (tm, LANES), 1)
    first_half = (lane & (HEAD_DIM // 2)) == 0
    low_head = lane < HEAD_DIM

    def rope(t):
        up = pltpu.roll(t, LANES - HEAD_DIM // 2, 1)
        dn = pltpu.roll(t, HEAD_DIM // 2, 1)
        return t * cs + jnp.where(first_half, up, dn) * sn

    scale = HEAD_DIM ** -0.5
    for j in range(nslab):
        q = rope(proj[:, pool_dim + LANES * j: pool_dim + LANES * (j + 1)]) * scale
        qa = jnp.where(low_head, q, 0.0).astype(BF16)
        qb = jnp.where(low_head, 0.0, q).astype(BF16)
        for b in range(nblk):
            rows = slice(ATT_BLOCK * b, ATT_BLOCK * (b + 1))
            qbuf[j, b, 0:ATT_BLOCK, :] = qa[rows]
            qbuf[j, b, ATT_BLOCK:2 * ATT_BLOCK, :] = qb[rows]
        yield
    kcol = pool_dim + q_dim
    kr = rope(proj[:, kcol:kcol + LANES])
    for b in range(nblk):
        kt_ref[:, ATT_BLOCK * b:ATT_BLOCK * (b + 1)] = kr[ATT_BLOCK * b:ATT_BLOCK * (b + 1)].T.astype(BF16)
    vt_ref[...] = proj[:, kcol + LANES:kcol + 2 * LANES].astype(BF16)
    yield

    tpos = tile_in_seq * tm + lax.broadcasted_iota(jnp.int32, (tm, 1), 0)
    for g, w in enumerate(POOL_WINDOWS):
        u_g = proj[:, LANES * g:LANES * (g + 1)]
        ubuf[g, halo:halo + tm, :] = u_g
        acc = u_g
        for k in range(1, w):
            acc = acc + ubuf[g, halo - k:halo - k + tm, :]
        cnt = jnp.minimum(tpos + 1, w).astype(F32)
        pooled = acc / cnt - u_g
        pm = jnp.dot(pooled.astype(BF16), poolw_ref[g], preferred_element_type=F32)
        mixbuf[:, LANES * g:LANES * (g + 1)] = (pm * pscale_ref[:, LANES * g:LANES * (g + 1)]).astype(BF16)
        ubuf[g, 0:halo, :] = ubuf[g, tm:tm + halo, :]
        yield


def _mix0_stage_b(x_ref, sink_ref, wout_ref, o_ref, kcarry, vcarry, buf, tile_in_seq, *, tm):
    qbuf, kt_ref, vt_ref, mixbuf = buf
    nblk = tm // ATT_BLOCK
    pool_dim, nslab = MIX_POOL_DIM, MIX_NSLAB

    qi = lax.broadcasted_iota(jnp.int32, (2 * ATT_BLOCK, 2 * ATT_BLOCK), 0) & (ATT_BLOCK - 1)
    kj = lax.broadcasted_iota(jnp.int32, (2 * ATT_BLOCK, 2 * ATT_BLOCK), 1)
    rel = qi + ATT_BLOCK - kj
    band = (rel >= 0) & (rel < ATT_BLOCK)
    first_lim = jnp.where(tile_in_seq == 0, ATT_BLOCK, 0)
    row2 = lax.broadcasted_iota(jnp.int32, (2 * ATT_BLOCK, 1), 0)
    lane_o = lax.broadcasted_iota(jnp.int32, (ATT_BLOCK, LANES), 1)
    for b in range(nblk):
        if b == 0:
            kk = jnp.concatenate([kcarry[...], kt_ref[:, 0:ATT_BLOCK]], axis=1)
            vv = jnp.concatenate([vcarry[...], vt_ref[0:ATT_BLOCK, :]], axis=0)
        else:
            kk = kt_ref[:, ATT_BLOCK * (b - 1):ATT_BLOCK * (b + 1)]
            vv = vt_ref[ATT_BLOCK * (b - 1):ATT_BLOCK * (b + 1), :]
        mask = (band & (kj >= first_lim)) if b == 0 else band
        for j in range(nslab):
            s = jnp.dot(qbuf[j, b], kk, preferred_element_type=F32)
            s = jnp.where(mask, s, -jnp.inf)
            sink = jnp.where(row2 < ATT_BLOCK, sink_ref[j], sink_ref[j + GQ])
            m = jnp.maximum(jnp.max(s, axis=-1, keepdims=True), sink)
            e = jnp.exp(s - m)
            den = jnp.sum(e, axis=-1, keepdims=True) + jnp.exp(sink - m)
            pv = jnp.dot(e.astype(BF16), vv, preferred_element_type=F32) / den
            o = jnp.where(lane_o < HEAD_DIM, pv[0:ATT_BLOCK], pv[ATT_BLOCK:2 * ATT_BLOCK])
            mixbuf[ATT_BLOCK * b:ATT_BLOCK * (b + 1),
                   pool_dim + LANES * j:pool_dim + LANES * (j + 1)] = o.astype(BF16)
            yield
    kcarry[...] = kt_ref[:, tm - ATT_BLOCK:tm]
    vcarry[...] = vt_ref[tm - ATT_BLOCK:tm, :]

    o_ref[...] = x_ref[...] + jnp.dot(mixbuf[...], wout_ref[...], preferred_element_type=F32)


def _round_robin(stages):
    stages = list(stages)
    while stages:
        for st in list(stages):
            if next(st, StopIteration) is StopIteration:
                stages.remove(st)


def _mix0_body(xa_ref, xb_ref, pos_ref, nw_ref, win_ref, invf_ref, sgn_ref, poolw_ref, pscale_ref,
               sink_ref, wout_ref, o_ref, ubuf, kcarry, vcarry, *bufs, tm, tiles_per_seq):
    i = pl.program_id(0)
    nbuf = len(bufs) // 2
    buf_even, buf_odd = bufs[:nbuf], bufs[nbuf:]
    halo = max(POOL_WINDOWS)
    tile_a = lax.rem(i, tiles_per_seq)
    tile_b = lax.rem(i - 1, tiles_per_seq)

    @pl.when(i == 0)
    def _():
        for r in buf_odd + (kcarry, vcarry):
            r[...] = jnp.zeros(r.shape, r.dtype)

    @pl.when(tile_a == 0)
    def _():
        ubuf[:, 0:halo, :] = jnp.zeros((len(POOL_WINDOWS), halo, LANES), F32)

    def both(buf_a, buf_b):
        _round_robin([
            _mix0_stage_a(xa_ref, pos_ref, nw_ref, win_ref, invf_ref, sgn_ref, poolw_ref, pscale_ref,
                          ubuf, buf_a, tile_a, tm=tm),
            _mix0_stage_b(xb_ref, sink_ref, wout_ref, o_ref, kcarry, vcarry, buf_b, tile_b, tm=tm),
        ])

    @pl.when(lax.rem(i, 2) == 0)
    def _():
        both(buf_even, buf_odd)

    @pl.when(lax.rem(i, 2) == 1)
    def _():
        both(buf_odd, buf_even)


def _mix0_layer(xt, pos, nw, w_in, pool_w, pool_scale, sinks, w_out, *, seq, tm):
    t, d = xt.shape
    pool_dim = LANES * len(POOL_WINDOWS)
    q_dim = N_HEADS * HEAD_DIM
    nslab = q_dim // LANES
    wq = w_in[:, pool_dim:pool_dim + q_dim].reshape(d, N_KV_HEADS, GQ, HEAD_DIM)
    wq = jnp.transpose(wq, (0, 2, 1, 3)).reshape(d, q_dim)
    w_in_p = jnp.concatenate([w_in[:, :pool_dim], wq, w_in[:, pool_dim + q_dim:]], axis=1).astype(BF16)
    wo = w_out[pool_dim:].reshape(N_KV_HEADS, GQ, HEAD_DIM, d)
    wo = jnp.transpose(wo, (1, 0, 2, 3)).reshape(q_dim, d)
    w_out_p = jnp.concatenate([w_out[:pool_dim], wo], axis=0).astype(BF16)
    nfreq = HEAD_DIM // 2
    ngrp = LANES // nfreq
    inv_freq = ROPE_THETA ** (-jnp.arange(0, HEAD_DIM, 2, dtype=F32) / HEAD_DIM)
    invf = jnp.tile(inv_freq, ngrp)[None, :]
    sgn = jnp.asarray(np.where((np.arange(LANES) % HEAD_DIM) < nfreq, -1.0, 1.0)[None, :], F32)
    pos_d = jnp.transpose(pos.reshape(t // tm, ngrp, tm // ngrp), (0, 2, 1))
    pos_d = jnp.repeat(pos_d, nfreq, axis=2).reshape(t // ngrp, LANES)
    mix_in = w_in_p.shape[1]
    nblk = tm // ATT_BLOCK
    halo = max(POOL_WINDOWS)
    n_tiles = t // tm
    body = functools.partial(_mix0_body, tm=tm, tiles_per_seq=seq // tm)
    staged = [
        pltpu.VMEM((nslab, nblk, 2 * ATT_BLOCK, LANES), BF16),
        pltpu.VMEM((LANES, tm), BF16),
        pltpu.VMEM((tm, LANES), BF16),
        pltpu.VMEM((tm, pool_dim + q_dim), BF16),
    ]
    return pl.pallas_call(
        body,
        out_shape=jax.ShapeDtypeStruct((t, d), F32),
        grid=(n_tiles + 1,),
        in_specs=[
            pl.BlockSpec((tm, d), lambda i: (jnp.minimum(i, n_tiles - 1), 0)),
            pl.BlockSpec((tm, d), lambda i: (jnp.maximum(i - 1, 0), 0)),
            pl.BlockSpec((tm // ngrp, LANES), lambda i: (jnp.minimum(i, n_tiles - 1), 0)),
            _const_spec((1, d)),
            _const_spec((d, mix_in)),
            _const_spec((1, LANES)),
            _const_spec((1, LANES)),
            _const_spec(pool_w.shape),
            _const_spec((1, pool_dim)),
            pl.BlockSpec(memory_space=pltpu.SMEM),
            _const_spec((pool_dim + q_dim, d)),
        ],
        out_specs=pl.BlockSpec((tm, d), lambda i: (jnp.maximum(i - 1, 0), 0)),
        scratch_shapes=[
            pltpu.VMEM((len(POOL_WINDOWS), halo + tm, LANES), F32),
            pltpu.VMEM((ATT_BLOCK, LANES), BF16),
            pltpu.VMEM((ATT_BLOCK, LANES), BF16),
        ] + staged + staged,
        compiler_params=pltpu.CompilerParams(dimension_semantics=("arbitrary",),
                                             vmem_limit_bytes=VMEM_LIMIT_BYTES),
        name="pool_swa_mixer",
    )(xt, xt, pos_d, nw.reshape(1, d), w_in_p, invf, sgn, pool_w.astype(BF16),
      pool_scale.reshape(1, pool_dim), sinks, w_out_p)


SSD_D_INNER = SSM_HEADS * SSM_HEAD_DIM
SSD_BC_DIM = SSM_GROUPS * SSM_STATE
SSD_N_XS = SSD_D_INNER // LANES
SSD_N_B = SSD_BC_DIM // LANES


def _ssd_stage_a1(x_ref, nw_ref, wxbc_ref, wdt_ref, dtb_ref, alog_ref, s_ref, halo_ref,
                  hb_ref, dt_ref, a_ref, *, tm):
    cw_cols = 4 * LANES
    slabs_per_chunk = cw_cols // LANES
    n_cchunks = (SSD_N_XS + 2 * SSD_N_B) // slabs_per_chunk
    first_bc = SSD_N_XS // slabs_per_chunk

    hb_ref[...] = _rms(x_ref[...], nw_ref[...], NORM_EPS).astype(BF16)
    yield
    for cc in list(range(first_bc, n_cchunks)) + list(range(first_bc)):
        hid = jnp.dot(hb_ref[...], wxbc_ref[:, cw_cols * cc:cw_cols * (cc + 1)],
                      preferred_element_type=F32)
        for jj in range(slabs_per_chunk):
            j = slabs_per_chunk * cc + jj
            x0 = hid[:, LANES * jj:LANES * (jj + 1)]
            s_ref[j, 0:SUBLANES, :] = halo_ref[j]
            s_ref[j, SUBLANES:SUBLANES + tm, :] = x0
            halo_ref[j] = x0[tm - SUBLANES:tm]
        yield
    dt = jax.nn.softplus(jnp.dot(hb_ref[...], wdt_ref[...], preferred_element_type=F32) + dtb_ref[...])
    dt_ref[...] = dt
    a_ref[...] = dt * (-jnp.exp(alog_ref[...]))


def _ssd_stage_a2(cw_ref, cb_ref, s_ref, xs_ref, xsb_ref, bt_ref, c_ref, *, tm):
    n_xs, n_b = SSD_N_XS, SSD_N_B
    nchunk = tm // SSM_CHUNK
    L = SSM_CHUNK
    order = list(range(n_xs, n_xs + 2 * n_b)) + list(range(n_xs))
    for idx, j in enumerate(order):
        cols = slice(LANES * j, LANES * (j + 1))
        y = s_ref[j, SUBLANES:SUBLANES + tm, :] * cw_ref[SSM_CONV - 1:SSM_CONV, cols] + cb_ref[:, cols]
        for k in range(1, SSM_CONV):
            y = y + (s_ref[j, SUBLANES - k:SUBLANES - k + tm, :]
                     * cw_ref[SSM_CONV - 1 - k:SSM_CONV - k, cols])
        y = _silu(y)
        if j < n_xs:
            xs_ref[j] = y
            xsb_ref[j] = y.astype(BF16)
        elif j < n_xs + n_b:
            for c in range(nchunk):
                bt_ref[j - n_xs, c] = y[L * c:L * (c + 1)].T.astype(BF16)
        else:
            c_ref[j - n_xs - n_b] = y.astype(BF16)
        if idx % 4 == 3:
            yield


def _ssd_stage_b1(dvec_ref, tril_ref, e3_ref, y_ref, state_ref,
                  xs_ref, xsb_ref, bt_ref, c_ref, dt_ref, a_ref, *, tm):
    d_inner, n_xs = SSD_D_INNER, SSD_N_XS
    nchunk = tm // SSM_CHUNK
    L = SSM_CHUNK
    heads_per_group = SSM_HEADS // SSM_GROUPS
    gw = heads_per_group * SSM_HEAD_DIM

    dt = dt_ref[...]
    a = a_ref[...]
    lane_t = lax.broadcasted_iota(jnp.int32, (tm, LANES), 1)
    hi, mid, lo = _split3_f32(dt)
    dt3 = jnp.where(lane_t < SSM_HEADS, hi, jnp.where(lane_t < 2 * SSM_HEADS, mid, lo)).astype(BF16)
    dtb = jnp.dot(dt3, e3_ref[...], preferred_element_type=F32)
    yield

    tril = tril_ref[...]
    li = lax.broadcasted_iota(jnp.int32, (L, L), 0)
    si = lax.broadcasted_iota(jnp.int32, (L, L), 1)
    causal = si <= li
    low_head = lax.broadcasted_iota(jnp.int32, (L, LANES), 1) < SSM_HEAD_DIM
    head_of_lane = lax.broadcasted_iota(jnp.int32, (L, gw), 1) // SSM_HEAD_DIM
    zero_blk = jnp.zeros((SSM_STATE, L), BF16)

    for c in range(nchunk):
        rows = slice(L * c, L * (c + 1))
        hi, mid, lo = _split3_f32(a[rows])
        a_cs = (jnp.dot(tril, hi.astype(BF16), preferred_element_type=F32)
                + jnp.dot(tril, mid.astype(BF16), preferred_element_type=F32)
                + jnp.dot(tril, lo.astype(BF16), preferred_element_type=F32))
        a_cst = a_cs.T
        dtt = dt[rows].T

        for gp in range(SSM_GROUPS // 2):
            g0 = 2 * gp
            ccat = jnp.concatenate([c_ref[g0, rows, :], c_ref[g0 + 1, rows, :]], axis=1)
            bd = jnp.concatenate(
                [jnp.concatenate([bt_ref[g0, c], zero_blk], axis=1),
                 jnp.concatenate([zero_blk, bt_ref[g0 + 1, c]], axis=1)], axis=0)
            cbp = jnp.dot(ccat, bd, preferred_element_type=F32)
            for gi in range(2):
                g = g0 + gi
                cb = cbp[:, L * gi:L * (gi + 1)]
                ms, eas, xds = [], [], []
                for pair in range(heads_per_group // 2):
                    slab = 2 * g + pair
                    acols = []
                    for r in range(2):
                        h = heads_per_group * g + 2 * pair + r
                        acol = jnp.broadcast_to(a_cs[:, h:h + 1], (L, LANES))
                        seg = acol - a_cst[h:h + 1, :]
                        lm = jnp.exp(jnp.where(causal, seg, -jnp.inf))
                        ms.append((cb * lm * dtt[h:h + 1, :]).astype(BF16))
                        acols.append(acol)
                    sel = jnp.where(low_head, acols[0], acols[1])
                    eas.append(jnp.exp(sel))
                    dec = jnp.exp(sel[L - 1:L, :] - sel)
                    xds.append((xs_ref[slab, rows, :] * dtb[rows, LANES * slab:LANES * (slab + 1)] * dec)
                               .astype(BF16))
                ea_g = jnp.concatenate(eas, axis=1)
                cc_ = c_ref[g, rows, :]
                prev = state_ref[g]
                yoff = jnp.dot(cc_, prev.astype(BF16), preferred_element_type=F32) * ea_g
                st_new = jnp.dot(bt_ref[g, c], jnp.concatenate(xds, axis=1), preferred_element_type=F32)
                state_ref[g] = prev * ea_g[L - 1:L, :] + st_new
                xg = jnp.concatenate([xsb_ref[2 * g, rows, :], xsb_ref[2 * g + 1, rows, :]], axis=1)
                zero_x = jnp.zeros_like(xg)
                rhs = jnp.concatenate([jnp.where(head_of_lane == r, xg, zero_x)
                                       for r in range(heads_per_group)], axis=0)
                yd = jnp.dot(jnp.concatenate(ms, axis=1), rhs, preferred_element_type=F32) + yoff
                for pair in range(heads_per_group // 2):
                    slab = 2 * g + pair
                    y_ref[slab, rows, :] = (yd[:, LANES * pair:LANES * (pair + 1)]
                                            + dvec_ref[slab:slab + 1, :] * xs_ref[slab, rows, :])
            yield


def _ssd_stage_b2(x_ref, hb_ref, wz_ref, gnw_ref, wout_ref, o_ref, y_ref, *, tm):
    d_inner, n_xs = SSD_D_INNER, SSD_N_XS
    ssq = jnp.zeros((tm, 1), F32)
    zc = 4 * LANES
    for cc in range(d_inner // zc):
        z = jnp.dot(hb_ref[...], wz_ref[:, zc * cc:zc * (cc + 1)], preferred_element_type=F32)
        for jj in range(zc // LANES):
            j = (zc // LANES) * cc + jj
            yg = y_ref[j] * _silu(z[:, LANES * jj:LANES * (jj + 1)])
            y_ref[j] = yg
            ssq = ssq + jnp.sum(yg * yg, axis=-1, keepdims=True)
        yield
    rinv = lax.rsqrt(ssq * (1.0 / d_inner) + SSM_NORM_EPS)
    yn = jnp.concatenate(
        [(y_ref[j] * rinv * gnw_ref[:, LANES * j:LANES * (j + 1)]).astype(BF16) for j in range(n_xs)],
        axis=1)
    o_ref[...] = x_ref[...] + jnp.dot(yn, wout_ref[...], preferred_element_type=F32)


def _ssd_body(xa_ref, xb_ref, nw_ref, win_hbm, cw_ref, cb_ref, dtb_ref, alog_ref,
              dvec_ref, gnw_ref, wout_hbm, tril_ref, e3_ref, o_ref,
              s_ref, halo_ref, y_ref, state_ref, wz_ref, wxbc_ref, wdt_ref, wout_ref,
              w_stage, sem, hb_st, dt_st, a_st, hb_p, dt_p, a_p, xs_ref, xsb_ref, bt_ref, c_ref,
              *, layer, tm, tiles_per_seq):
    i = pl.program_id(0)

    @pl.when(i == 0)
    def _():
        for r in (hb_st, dt_st, a_st, xs_ref, xsb_ref, bt_ref, c_ref):
            r[...] = jnp.zeros(r.shape, r.dtype)
        d_inner = wz_ref.shape[1]
        conv_dim = wxbc_ref.shape[1]
        rows = w_stage.shape[1]
        _stream_cast(win_hbm.at[layer, pl.ds(0, d_inner), :], wz_ref, w_stage, sem, rows, transpose=True)
        _stream_cast(win_hbm.at[layer, pl.ds(d_inner, conv_dim), :], wxbc_ref, w_stage, sem, rows,
                     transpose=True)
        _stream_cast(wout_hbm.at[layer], wout_ref, w_stage, sem, rows)
        dt_copy = pltpu.make_async_copy(win_hbm.at[layer, pl.ds(d_inner + conv_dim, SSM_HEADS), :],
                                        w_stage.at[0, pl.ds(0, SSM_HEADS), :], sem.at[0])
        dt_copy.start()
        dt_copy.wait()
        wd = w_stage[0, 0:SSM_HEADS, :]
        copies = LANES // SSM_HEADS - 1
        wdt_ref[...] = jnp.concatenate([wd] * copies + [jnp.zeros_like(wd)], axis=0).T.astype(BF16)

    @pl.when(lax.rem(i, tiles_per_seq) == 0)
    def _():
        halo_ref[...] = jnp.zeros(halo_ref.shape, halo_ref.dtype)

    @pl.when((i == 0) | (lax.rem(i - 1, tiles_per_seq) == 0))
    def _():
        state_ref[...] = jnp.zeros(state_ref.shape, state_ref.dtype)

    hb_p[...] = hb_st[...]
    dt_p[...] = dt_st[...]
    a_p[...] = a_st[...]
    _round_robin([
        _ssd_stage_a1(xa_ref, nw_ref, wxbc_ref, wdt_ref, dtb_ref, alog_ref, s_ref, halo_ref,
                      hb_st, dt_st, a_st, tm=tm),
        _ssd_stage_b1(dvec_ref, tril_ref, e3_ref, y_ref, state_ref,
                      xs_ref, xsb_ref, bt_ref, c_ref, dt_p, a_p, tm=tm),
    ])
    _round_robin([
        _ssd_stage_a2(cw_ref, cb_ref, s_ref, xs_ref, xsb_ref, bt_ref, c_ref, tm=tm),
        _ssd_stage_b2(xb_ref, hb_p, wz_ref, gnw_ref, wout_ref, o_ref, y_ref, tm=tm),
    ])


def _ssd_layer(xt, nw, w_in_all, conv_w, conv_b, dt_bias, a_log, d_skip, norm_w, w_out_all, *, layer, seq, tm):
    t, d = xt.shape
    d_inner = SSM_HEADS * SSM_HEAD_DIM
    bc_dim = SSM_GROUPS * SSM_STATE
    conv_dim = d_inner + 2 * bc_dim
    n_xs = d_inner // LANES
    n_b = bc_dim // LANES
    nchunk = tm // SSM_CHUNK
    L = SSM_CHUNK
    copies = 3
    pad = LANES - copies * SSM_HEADS

    def lanes3(v):
        return jnp.concatenate([v] * copies + [jnp.zeros(v.shape[:-1] + (pad,), v.dtype)], axis=-1)

    w_in_t = jnp.swapaxes(w_in_all, 1, 2)
    dtb = lanes3(dt_bias.reshape(1, SSM_HEADS))
    alog = lanes3(a_log.reshape(1, SSM_HEADS))
    dvec = jnp.repeat(d_skip, SSM_HEAD_DIM).reshape(n_xs, LANES)
    tril = jnp.asarray(np.tril(np.ones((L, L), np.float32)), BF16)
    rows_h = np.arange(LANES) % SSM_HEADS
    valid = (np.arange(LANES) < copies * SSM_HEADS)[:, None]
    e3 = jnp.asarray(((rows_h[:, None] == (np.arange(d_inner) // SSM_HEAD_DIM)[None, :]) & valid)
                     .astype(np.float32), BF16)
    n_tiles = t // tm
    body = functools.partial(_ssd_body, layer=layer, tm=tm, tiles_per_seq=seq // tm)
    stage_slots, stage_rows = WEIGHT_STAGE_SLOTS, WEIGHT_STAGE_ELEMS // d
    return pl.pallas_call(
        body,
        out_shape=jax.ShapeDtypeStruct((t, d), F32),
        grid=(n_tiles + 1,),
        in_specs=[
            pl.BlockSpec((tm, d), lambda i: (jnp.minimum(i, n_tiles - 1), 0)),
            pl.BlockSpec((tm, d), lambda i: (jnp.maximum(i - 1, 0), 0)),
            _const_spec((1, d)),
            pl.BlockSpec(memory_space=pl.ANY),
            _const_spec((SSM_CONV, conv_dim)),
            _const_spec((1, conv_dim)),
            _const_spec((1, LANES)),
            _const_spec((1, LANES)),
            _const_spec((n_xs, LANES)),
            _const_spec((1, d_inner)),
            pl.BlockSpec(memory_space=pl.ANY),
            _const_spec((L, L)),
            _const_spec((LANES, d_inner)),
        ],
        out_specs=pl.BlockSpec((tm, d), lambda i: (jnp.maximum(i - 1, 0), 0)),
        scratch_shapes=[
            pltpu.VMEM((n_xs + 2 * n_b, SUBLANES + tm, LANES), F32),
            pltpu.VMEM((n_xs + 2 * n_b, SUBLANES, LANES), F32),
            pltpu.VMEM((n_xs, tm, LANES), F32),
            pltpu.VMEM((SSM_GROUPS, SSM_STATE, 2 * LANES), F32),
            pltpu.VMEM((d, d_inner), BF16),
            pltpu.VMEM((d, conv_dim), BF16),
            pltpu.VMEM((d, LANES), BF16),
            pltpu.VMEM((d_inner, d), BF16),
            pltpu.VMEM((stage_slots, stage_rows, d), F32),
            pltpu.SemaphoreType.DMA((stage_slots,)),
            pltpu.VMEM((tm, d), BF16),
            pltpu.VMEM((tm, LANES), F32),
            pltpu.VMEM((tm, LANES), F32),
            pltpu.VMEM((tm, d), BF16),
            pltpu.VMEM((tm, LANES), F32),
            pltpu.VMEM((tm, LANES), F32),
            pltpu.VMEM((n_xs, tm, LANES), F32),
            pltpu.VMEM((n_xs, tm, LANES), BF16),
            pltpu.VMEM((n_b, nchunk, SSM_STATE, L), BF16),
            pltpu.VMEM((n_b, tm, LANES), BF16),
        ],
        compiler_params=pltpu.CompilerParams(dimension_semantics=("arbitrary",),
                                             vmem_limit_bytes=VMEM_LIMIT_BYTES),
        name="ssd_mixer",
    )(xt, xt, nw.reshape(1, d), w_in_t, conv_w, conv_b.reshape(1, conv_dim), dtb, alog, dvec,
      norm_w.reshape(1, d_inner), w_out_all, tril, e3)


def _ffn_body(x_ref, nw_ref, wup_hbm, cw_ref, cb_ref, wdn_hbm, fnw_ref, o_ref,
              hb_ref, s_ref, halo_ref, act_ref, wu_ref, wg_ref, wdn_ref,
              up_stage, dn_stage, sem, *, layer, tm, fc, tiles_per_seq, final_norm):
    i = pl.program_id(0)
    d_ff = act_ref.shape[1]
    half_slabs = fc // LANES

    @pl.when(i == 0)
    def _():
        _stream_cast(wup_hbm.at[layer, :, pl.ds(0, d_ff)], wu_ref, up_stage, sem, up_stage.shape[1])
        _stream_cast(wup_hbm.at[layer, :, pl.ds(d_ff, d_ff)], wg_ref, up_stage, sem, up_stage.shape[1])
        _stream_cast(wdn_hbm.at[layer], wdn_ref, dn_stage, sem, dn_stage.shape[1])

    @pl.when(lax.rem(i, tiles_per_seq) == 0)
    def _():
        halo_ref[...] = jnp.zeros(halo_ref.shape, halo_ref.dtype)

    x = x_ref[...]
    hb_ref[...] = _rms(x, nw_ref[...], NORM_EPS).astype(BF16)

    for c in range(d_ff // fc):
        slot = c % 2
        ys = []
        for half, w_ref in enumerate((wu_ref, wg_ref)):
            hid = jnp.dot(hb_ref[...], w_ref[:, fc * c:fc * (c + 1)], preferred_element_type=F32)
            for jj in range(half_slabs):
                j = half * half_slabs + jj
                col = half * d_ff + fc * c + LANES * jj
                gs = col // LANES
                x0 = hid[:, LANES * jj:LANES * (jj + 1)]
                s_ref[slot, j, 0:SUBLANES, :] = halo_ref[gs]
                s_ref[slot, j, SUBLANES:SUBLANES + tm, :] = x0
                halo_ref[gs] = x0[tm - SUBLANES:tm]
                y = x0 * cw_ref[FFN_CONV - 1:FFN_CONV, col:col + LANES] + cb_ref[:, col:col + LANES]
                for k in range(1, FFN_CONV):
                    y = y + (s_ref[slot, j, SUBLANES - k:SUBLANES - k + tm, :]
                             * cw_ref[FFN_CONV - 1 - k:FFN_CONV - k, col:col + LANES])
                ys.append(y)
        for jj in range(half_slabs):
            u = ys[jj]
            g = ys[half_slabs + jj]
            act_ref[:, fc * c + LANES * jj:fc * c + LANES * (jj + 1)] = (_silu(g) * u).astype(BF16)

    out = x + jnp.dot(act_ref[...], wdn_ref[...], preferred_element_type=F32)
    if final_norm:
        out = _rms(out, fnw_ref[...], NORM_EPS)
    o_ref[...] = out


def _ffn_layer(xt, nw, w_up_all, conv_w, conv_b, w_down_all, final_w, *, layer, seq, tm, fc, final_norm):
    t, d = xt.shape
    d_ff = w_down_all.shape[1]
    n_tiles = t // tm
    stage_slots = WEIGHT_STAGE_SLOTS
    up_rows = LANES
    dn_rows = WEIGHT_STAGE_ELEMS // d
    assert d % up_rows == 0 and d_ff % dn_rows == 0
    body = functools.partial(_ffn_body, layer=layer, tm=tm, fc=fc, tiles_per_seq=seq // tm,
                             final_norm=final_norm)
    return pl.pallas_call(
        body,
        out_shape=jax.ShapeDtypeStruct((t, d), F32),
        grid=(n_tiles,),
        in_specs=[
            pl.BlockSpec((tm, d), lambda i: (i, 0)),
            _const_spec((1, d)),
            pl.BlockSpec(memory_space=pl.ANY),
            _const_spec((FFN_CONV, 2 * d_ff)),
            _const_spec((1, 2 * d_ff)),
            pl.BlockSpec(memory_space=pl.ANY),
            _const_spec((1, d)),
        ],
        out_specs=pl.BlockSpec((tm, d), lambda i: (i, 0)),
        scratch_shapes=[
            pltpu.VMEM((tm, d), BF16),
            pltpu.VMEM((2, 2 * fc // LANES, SUBLANES + tm, LANES), F32),
            pltpu.VMEM((2 * d_ff // LANES, SUBLANES, LANES), F32),
            pltpu.VMEM((tm, d_ff), BF16),
            pltpu.VMEM((d, d_ff), BF16),
            pltpu.VMEM((d, d_ff), BF16),
            pltpu.VMEM((d_ff, d), BF16),
            pltpu.VMEM((stage_slots, up_rows, d_ff), F32),
            pltpu.VMEM((stage_slots, dn_rows, d), F32),
            pltpu.SemaphoreType.DMA((stage_slots,)),
        ],
        compiler_params=pltpu.CompilerParams(dimension_semantics=("arbitrary",),
                                             vmem_limit_bytes=VMEM_LIMIT_BYTES),
        name="conv_ffn_final" if final_norm else "conv_ffn",
    )(xt, nw.reshape(1, d), w_up_all, conv_w, conv_b.reshape(1, 2 * d_ff), w_down_all,
      final_w.reshape(1, d))


def kernel(x, positions, norm_mix, norm_ffn, norm_final, mix_w_in, pool_w, pool_scale, attn_sinks,
           mix_w_out, ssm_w_in, ssm_conv_w, ssm_conv_b, ssm_dt_bias, ssm_A_log, ssm_D, ssm_norm,
           ssm_w_out, ffn_w_up, ffn_conv_w, ffn_conv_b, ffn_w_down):
    b, s, d = x.shape
    depth = norm_mix.shape[0]
    xt = x.reshape(b * s, d)
    pos = positions.reshape(b * s)
    for i in range(depth):
        j = i // 2
        if i % 2 == 0:
            xt = _mix0_layer(xt, pos, norm_mix[i], mix_w_in[j], pool_w[j], pool_scale[j], attn_sinks[j],
                             mix_w_out[j], seq=s, tm=MIX_TILE)
        else:
            xt = _ssd_layer(xt, norm_mix[i], ssm_w_in, ssm_conv_w[j], ssm_conv_b[j], ssm_dt_bias[j],
                            ssm_A_log[j], ssm_D[j], ssm_norm[j], ssm_w_out, layer=j, seq=s, tm=SSD_TILE)
        xt = _ffn_layer(xt, norm_ffn[i], ffn_w_up, ffn_conv_w[i], ffn_conv_b[i], ffn_w_down,
                        norm_final, layer=i, seq=s, tm=FFN_TILE, fc=FFN_COL_CHUNK,
                        final_norm=(i == depth - 1))
    return xt.reshape(b, s, d)
```

```python
import functools
import math

import numpy as np
import jax
import jax.numpy as jnp
from jax import lax
from jax.experimental import pallas as pl
from jax.experimental.pallas import tpu as pltpu

HEAD_DIM = 64
N_HEADS = 8
N_KV_HEADS = 2
GQ = N_HEADS // N_KV_HEADS
ATT_BLOCK = 128
ROPE_THETA = 10000.0
POOL_WINDOWS = (2, 4, 8, 16)
SSM_HEAD_DIM = 64
SSM_HEADS = 32
SSM_GROUPS = 8
SSM_STATE = 128
SSM_CONV = 4
SSM_CHUNK = 128
FFN_CONV = 3
NORM_EPS = 1e-6
SSM_NORM_EPS = 1e-5

LANES = 128
SUBLANES = 8
VMEM_LIMIT_BYTES = 56 * 1024 * 1024
WEIGHT_STAGE_SLOTS = 4
WEIGHT_STAGE_ELEMS = 256 * 1024

MIX_TILE = 512
SSD_TILE = 256
FFN_TILE = 512
FFN_COL_CHUNK = 256

F32 = jnp.float32
BF16 = jnp.bfloat16


def _rms(x, w, eps):
    ms = jnp.mean(x * x, axis=-1, keepdims=True)
    return x * lax.rsqrt(ms + eps) * w


def _silu(x):
    return x * jax.nn.sigmoid(x)


def _const_spec(shape):
    nd = len(shape)
    return pl.BlockSpec(shape, lambda i: (0,) * nd, pipeline_mode=pl.Buffered(1))


def _stream_cast(src, dst_ref, stage_ref, sem, chunk_rows, transpose=False):
    rows = dst_ref.shape[1] if transpose else dst_ref.shape[0]
    n = rows // chunk_rows
    slots = stage_ref.shape[0]

    def copy(k):
        return pltpu.make_async_copy(src.at[pl.ds(k * chunk_rows, chunk_rows), :],
                                     stage_ref.at[k % slots], sem.at[k % slots])

    for k in range(min(slots - 1, n)):
        copy(k).start(priority=k % 2)
    for k in range(n):
        if k + slots - 1 < n:
            copy(k + slots - 1).start(priority=(k + slots - 1) % 2)
        copy(k).wait()
        if transpose:
            dst_ref[:, chunk_rows * k:chunk_rows * (k + 1)] = stage_ref[k % slots].T.astype(BF16)
        else:
            dst_ref[chunk_rows * k:chunk_rows * (k + 1), :] = stage_ref[k % slots].astype(BF16)


def _split3_f32(x):
    hi = x.astype(BF16).astype(F32)
    r1 = x - hi
    mid = r1.astype(BF16).astype(F32)
    lo = (r1 - mid).astype(BF16).astype(F32)
    return hi, mid, lo


MIX_POOL_DIM = LANES * len(POOL_WINDOWS)
MIX_Q_DIM = N_HEADS * HEAD_DIM
MIX_NSLAB = MIX_Q_DIM // LANES


def _mix0_stage_a1(x_ref, nw_ref, win_ref, proj_ref, *, tm):
    hb = _rms(x_ref[...], nw_ref[...], NORM_EPS).astype(BF16)
    width = 2 * LANES
    for c in range(proj_ref.shape[1] // width):
        proj_ref[:, width * c:width * (c + 1)] = jnp.dot(hb, win_ref[:, width * c:width * (c + 1)],
                                                         preferred_element_type=F32)
        yield


def _mix0_stage_a2(pos_ref, invf_ref, sgn_ref, poolw_ref, pscale_ref, proj_ref,
                   ubuf, qbuf, kt_ref, vt_ref, pooled_ref, tile_in_seq, *, tm):
    nblk = tm // ATT_BLOCK
    halo = max(POOL_WINDOWS)
    pool_dim, q_dim, nslab = MIX_POOL_DIM, MIX_Q_DIM, MIX_NSLAB

    nfreq = HEAD_DIM // 2
    ngrp = LANES // nfreq
    ang = pos_ref[...].astype(F32) * invf_ref[...]
    grp = lax.broadcasted_iota(jnp.int32, (tm // ngrp, LANES), 1) // nfreq

    def spread(dense):
        rolled = [dense] + [pltpu.roll(dense, nfreq * s, 1) for s in range(1, ngrp)]
        parts = []
        for k in range(ngrp):
            out = rolled[(-k) % ngrp]
            for g in range(1, ngrp):
                out = jnp.where(grp == g, rolled[(g - k) % ngrp], out)
            parts.append(out)
        return jnp.concatenate(parts, axis=0)

    cs = spread(jnp.cos(ang))
    sn = spread(jnp.sin(ang)) * sgn_ref[...]
    lane = lax.broadcasted_iota(jnp.int32, (tm, LANES), 1)
    first_half = (lane & (HEAD_DIM // 2)) == 0
    low_head = lane < HEAD_DIM

    def rope(t):
        up = pltpu.roll(t, LANES - HEAD_DIM // 2, 1)
        dn = pltpu.roll(t, HEAD_DIM // 2, 1)
        return t * cs + jnp.where(first_half, up, dn) * sn

    scale = HEAD_DIM ** -0.5
    for j in range(nslab):
        q = rope(proj_ref[:, pool_dim + LANES * j: pool_dim + LANES * (j + 1)]) * scale
        qa = jnp.where(low_head, q, 0.0).astype(BF16)
        qb = jnp.where(low_head, 0.0, q).astype(BF16)
        for b in range(nblk):
            rows = slice(ATT_BLOCK * b, ATT_BLOCK * (b + 1))
            qbuf[j, b, 0:ATT_BLOCK, :] = qa[rows]
            qbuf[j, b, ATT_BLOCK:2 * ATT_BLOCK, :] = qb[rows]
        yield
    kcol = pool_dim + q_dim
    kr = rope(proj_ref[:, kcol:kcol + LANES])
    for b in range(nblk):
        kt_ref[:, ATT_BLOCK * b:ATT_BLOCK * (b + 1)] = kr[ATT_BLOCK * b:ATT_BLOCK * (b + 1)].T.astype(BF16)
    vt_ref[...] = proj_ref[:, kcol + LANES:kcol + 2 * LANES].astype(BF16)
    yield

    tpos = tile_in_seq * tm + lax.broadcasted_iota(jnp.int32, (tm, 1), 0)
    for g, w in enumerate(POOL_WINDOWS):
        u_g = proj_ref[:, LANES * g:LANES * (g + 1)]
        ubuf[g, halo:halo + tm, :] = u_g
        acc = u_g
        for k in range(1, w):
            acc = acc + ubuf[g, halo - k:halo - k + tm, :]
        cnt = jnp.minimum(tpos + 1, w).astype(F32)
        pooled = acc / cnt - u_g
        pm = jnp.dot(pooled.astype(BF16), poolw_ref[g], preferred_element_type=F32)
        pooled_ref[:, LANES * g:LANES * (g + 1)] = (pm * pscale_ref[:, LANES * g:LANES * (g + 1)]).astype(BF16)
        ubuf[g, 0:halo, :] = ubuf[g, tm:tm + halo, :]
        yield


def _mix0_stage_b1(sink_ref, kcarry, vcarry, qbuf, kt_ref, vt_ref, mixbuf, tile_in_seq, *, tm):
    nblk = tm // ATT_BLOCK
    pool_dim, nslab = MIX_POOL_DIM, MIX_NSLAB

    qi = lax.broadcasted_iota(jnp.int32, (2 * ATT_BLOCK, 2 * ATT_BLOCK), 0) & (ATT_BLOCK - 1)
    kj = lax.broadcasted_iota(jnp.int32, (2 * ATT_BLOCK, 2 * ATT_BLOCK), 1)
    rel = qi + ATT_BLOCK - kj
    band = (rel >= 0) & (rel < ATT_BLOCK)
    first_lim = jnp.where(tile_in_seq == 0, ATT_BLOCK, 0)
    row2 = lax.broadcasted_iota(jnp.int32, (2 * ATT_BLOCK, 1), 0)
    lane_o = lax.broadcasted_iota(jnp.int32, (ATT_BLOCK, LANES), 1)
    for b in range(nblk):
        if b == 0:
            kk = jnp.concatenate([kcarry[...], kt_ref[:, 0:ATT_BLOCK]], axis=1)
            vv = jnp.concatenate([vcarry[...], vt_ref[0:ATT_BLOCK, :]], axis=0)
        else:
            kk = kt_ref[:, ATT_BLOCK * (b - 1):ATT_BLOCK * (b + 1)]
            vv = vt_ref[ATT_BLOCK * (b - 1):ATT_BLOCK * (b + 1), :]
        mask = (band & (kj >= first_lim)) if b == 0 else band
        for j in range(nslab):
            s = jnp.dot(qbuf[j, b], kk, preferred_element_type=F32)
            s = jnp.where(mask, s, -jnp.inf)
            sink = jnp.where(row2 < ATT_BLOCK, sink_ref[j], sink_ref[j + GQ])
            m = jnp.maximum(jnp.max(s, axis=-1, keepdims=True), sink)
            e = jnp.exp(s - m)
            den = jnp.sum(e, axis=-1, keepdims=True) + jnp.exp(sink - m)
            pv = jnp.dot(e.astype(BF16), vv, preferred_element_type=F32) / den
            o = jnp.where(lane_o < HEAD_DIM, pv[0:ATT_BLOCK], pv[ATT_BLOCK:2 * ATT_BLOCK])
            mixbuf[ATT_BLOCK * b:ATT_BLOCK * (b + 1),
                   pool_dim + LANES * j:pool_dim + LANES * (j + 1)] = o.astype(BF16)
            yield
    kcarry[...] = kt_ref[:, tm - ATT_BLOCK:tm]
    vcarry[...] = vt_ref[tm - ATT_BLOCK:tm, :]


def _mix0_stage_b2(x_ref, wout_ref, o_ref, mixbuf):
    o_ref[...] = x_ref[...] + jnp.dot(mixbuf[...], wout_ref[...], preferred_element_type=F32)
    yield


def _round_robin(stages):
    stages = list(stages)
    while stages:
        for st in list(stages):
            if next(st, StopIteration) is StopIteration:
                stages.remove(st)


def _mix0_body(xa_ref, xb_ref, pos_ref, nw_ref, win_ref, invf_ref, sgn_ref, poolw_ref, pscale_ref,
               sink_ref, wout_ref, o_ref, ubuf, kcarry, vcarry, proj_ref, qbuf, kt_ref, vt_ref,
               pooled_ref, mixbuf, *, tm, tiles_per_seq):
    i = pl.program_id(0)
    halo = max(POOL_WINDOWS)
    pool_dim = MIX_POOL_DIM
    tile_a = lax.rem(i, tiles_per_seq)
    tile_b = lax.rem(i - 1, tiles_per_seq)

    @pl.when(i == 0)
    def _():
        for r in (qbuf, kt_ref, vt_ref, pooled_ref, kcarry, vcarry):
            r[...] = jnp.zeros(r.shape, r.dtype)

    @pl.when(tile_a == 0)
    def _():
        ubuf[:, 0:halo, :] = jnp.zeros((len(POOL_WINDOWS), halo, LANES), F32)

    mixbuf[:, 0:pool_dim] = pooled_ref[...]
    _round_robin([
        _mix0_stage_a1(xa_ref, nw_ref, win_ref, proj_ref, tm=tm),
        _mix0_stage_b1(sink_ref, kcarry, vcarry, qbuf, kt_ref, vt_ref, mixbuf, tile_b, tm=tm),
    ])
    _round_robin([
        _mix0_stage_a2(pos_ref, invf_ref, sgn_ref, poolw_ref, pscale_ref, proj_ref,
                       ubuf, qbuf, kt_ref, vt_ref, pooled_ref, tile_a, tm=tm),
        _mix0_stage_b2(xb_ref, wout_ref, o_ref, mixbuf),
    ])


def _mix0_layer(xt, pos, nw, w_in, pool_w, pool_scale, sinks, w_out, *, seq, tm):
    t, d = xt.shape
    pool_dim = LANES * len(POOL_WINDOWS)
    q_dim = N_HEADS * HEAD_DIM
    nslab = q_dim // LANES
    wq = w_in[:, pool_dim:pool_dim + q_dim].reshape(d, N_KV_HEADS, GQ, HEAD_DIM)
    wq = jnp.transpose(wq, (0, 2, 1, 3)).reshape(d, q_dim)
    w_in_p = jnp.concatenate([w_in[:, :pool_dim], wq, w_in[:, pool_dim + q_dim:]], axis=1).astype(BF16)
    wo = w_out[pool_dim:].reshape(N_KV_HEADS, GQ, HEAD_DIM, d)
    wo = jnp.transpose(wo, (1, 0, 2, 3)).reshape(q_dim, d)
    w_out_p = jnp.concatenate([w_out[:pool_dim], wo], axis=0).astype(BF16)
    nfreq = HEAD_DIM // 2
    ngrp = LANES // nfreq
    inv_freq = ROPE_THETA ** (-jnp.arange(0, HEAD_DIM, 2, dtype=F32) / HEAD_DIM)
    invf = jnp.tile(inv_freq, ngrp)[None, :]
    sgn = jnp.asarray(np.where((np.arange(LANES) % HEAD_DIM) < nfreq, -1.0, 1.0)[None, :], F32)
    pos_d = jnp.transpose(pos.reshape(t // tm, ngrp, tm // ngrp), (0, 2, 1))
    pos_d = jnp.repeat(pos_d, nfreq, axis=2).reshape(t // ngrp, LANES)
    mix_in = w_in_p.shape[1]
    nblk = tm // ATT_BLOCK
    halo = max(POOL_WINDOWS)
    n_tiles = t // tm
    body = functools.partial(_mix0_body, tm=tm, tiles_per_seq=seq // tm)
    return pl.pallas_call(
        body,
        out_shape=jax.ShapeDtypeStruct((t, d), F32),
        grid=(n_tiles + 1,),
        in_specs=[
            pl.BlockSpec((tm, d), lambda i: (jnp.minimum(i, n_tiles - 1), 0)),
            pl.BlockSpec((tm, d), lambda i: (jnp.maximum(i - 1, 0), 0)),
            pl.BlockSpec((tm // ngrp, LANES), lambda i: (jnp.minimum(i, n_tiles - 1), 0)),
            _const_spec((1, d)),
            _const_spec((d, mix_in)),
            _const_spec((1, LANES)),
            _const_spec((1, LANES)),
            _const_spec(pool_w.shape),
            _const_spec((1, pool_dim)),
            pl.BlockSpec(memory_space=pltpu.SMEM),
            _const_spec((pool_dim + q_dim, d)),
        ],
        out_specs=pl.BlockSpec((tm, d), lambda i: (jnp.maximum(i - 1, 0), 0)),
        scratch_shapes=[
            pltpu.VMEM((len(POOL_WINDOWS), halo + tm, LANES), F32),
            pltpu.VMEM((ATT_BLOCK, LANES), BF16),
            pltpu.VMEM((ATT_BLOCK, LANES), BF16),
            pltpu.VMEM((tm, mix_in), F32),
            pltpu.VMEM((nslab, nblk, 2 * ATT_BLOCK, LANES), BF16),
            pltpu.VMEM((LANES, tm), BF16),
            pltpu.VMEM((tm, LANES), BF16),
            pltpu.VMEM((tm, pool_dim), BF16),
            pltpu.VMEM((tm, pool_dim + q_dim), BF16),
        ],
        compiler_params=pltpu.CompilerParams(dimension_semantics=("arbitrary",),
                                             vmem_limit_bytes=VMEM_LIMIT_BYTES),
        name="pool_swa_mixer",
    )(xt, xt, pos_d, nw.reshape(1, d), w_in_p, invf, sgn, pool_w.astype(BF16),
      pool_scale.reshape(1, pool_dim), sinks, w_out_p)


SSD_D_INNER = SSM_HEADS * SSM_HEAD_DIM
SSD_BC_DIM = SSM_GROUPS * SSM_STATE
SSD_N_XS = SSD_D_INNER // LANES
SSD_N_B = SSD_BC_DIM // LANES


def _ssd_stage_a1(x_ref, nw_ref, wxbc_ref, wdt_ref, dtb_ref, alog_ref, s_ref, halo_ref,
                  hb_ref, dt_ref, a_ref, *, tm):
    cw_cols = 4 * LANES
    slabs_per_chunk = cw_cols // LANES
    n_cchunks = (SSD_N_XS + 2 * SSD_N_B) // slabs_per_chunk
    first_bc = SSD_N_XS // slabs_per_chunk

    hb_ref[...] = _rms(x_ref[...], nw_ref[...], NORM_EPS).astype(BF16)
    yield
    for cc in list(range(first_bc, n_cchunks)) + list(range(first_bc)):
        hid = jnp.dot(hb_ref[...], wxbc_ref[:, cw_cols * cc:cw_cols * (cc + 1)],
                      preferred_element_type=F32)
        for jj in range(slabs_per_chunk):
            j = slabs_per_chunk * cc + jj
            x0 = hid[:, LANES * jj:LANES * (jj + 1)]
            s_ref[j, 0:SUBLANES, :] = halo_ref[j]
            s_ref[j, SUBLANES:SUBLANES + tm, :] = x0
            halo_ref[j] = x0[tm - SUBLANES:tm]
        yield
    dt = jax.nn.softplus(jnp.dot(hb_ref[...], wdt_ref[...], preferred_element_type=F32) + dtb_ref[...])
    dt_ref[...] = dt
    a_ref[...] = dt * (-jnp.exp(alog_ref[...]))


def _ssd_stage_a2(cw_ref, cb_ref, s_ref, xs_ref, xsb_ref, bt_ref, c_ref, *, tm):
    n_xs, n_b = SSD_N_XS, SSD_N_B
    nchunk = tm // SSM_CHUNK
    L = SSM_CHUNK
    order = list(range(n_xs, n_xs + 2 * n_b)) + list(range(n_xs))
    for idx, j in enumerate(order):
        cols = slice(LANES * j, LANES * (j + 1))
        y = s_ref[j, SUBLANES:SUBLANES + tm, :] * cw_ref[SSM_CONV - 1:SSM_CONV, cols] + cb_ref[:, cols]
        for k in range(1, SSM_CONV):
            y = y + (s_ref[j, SUBLANES - k:SUBLANES - k + tm, :]
                     * cw_ref[SSM_CONV - 1 - k:SSM_CONV - k, cols])
        y = _silu(y)
        if j < n_xs:
            xs_ref[j] = y
            xsb_ref[j] = y.astype(BF16)
        elif j < n_xs + n_b:
            for c in range(nchunk):
                bt_ref[j - n_xs, c] = y[L * c:L * (c + 1)].T.astype(BF16)
        else:
            c_ref[j - n_xs - n_b] = y.astype(BF16)
        if idx % 4 == 3:
            yield


def _ssd_stage_b1(dvec_ref, tril_ref, e3_ref, y_ref, state_ref,
                  xs_ref, xsb_ref, bt_ref, c_ref, dt_ref, a_ref, *, tm):
    d_inner, n_xs = SSD_D_INNER, SSD_N_XS
    nchunk = tm // SSM_CHUNK
    L = SSM_CHUNK
    heads_per_group = SSM_HEADS // SSM_GROUPS
    gw = heads_per_group * SSM_HEAD_DIM

    dt = dt_ref[...]
    a = a_ref[...]
    lane_t = lax.broadcasted_iota(jnp.int32, (tm, LANES), 1)
    hi, mid, lo = _split3_f32(dt)
    dt3 = jnp.where(lane_t < SSM_HEADS, hi, jnp.where(lane_t < 2 * SSM_HEADS, mid, lo)).astype(BF16)
    dtb = jnp.dot(dt3, e3_ref[...], preferred_element_type=F32)
    yield

    tril = tril_ref[...]
    li = lax.broadcasted_iota(jnp.int32, (L, L), 0)
    si = lax.broadcasted_iota(jnp.int32, (L, L), 1)
    causal = si <= li
    low_head = lax.broadcasted_iota(jnp.int32, (L, LANES), 1) < SSM_HEAD_DIM
    head_of_lane = lax.broadcasted_iota(jnp.int32, (L, gw), 1) // SSM_HEAD_DIM
    zero_blk = jnp.zeros((SSM_STATE, L), BF16)

    for c in range(nchunk):
        rows = slice(L * c, L * (c + 1))
        hi, mid, lo = _split3_f32(a[rows])
        a_cs = (jnp.dot(tril, hi.astype(BF16), preferred_element_type=F32)
                + jnp.dot(tril, mid.astype(BF16), preferred_element_type=F32)
                + jnp.dot(tril, lo.astype(BF16), preferred_element_type=F32))
        a_cst = a_cs.T
        dtt = dt[rows].T

        for gp in range(SSM_GROUPS // 2):
            g0 = 2 * gp
            ccat = jnp.concatenate([c_ref[g0, rows, :], c_ref[g0 + 1, rows, :]], axis=1)
            bd = jnp.concatenate(
                [jnp.concatenate([bt_ref[g0, c], zero_blk], axis=1),
                 jnp.concatenate([zero_blk, bt_ref[g0 + 1, c]], axis=1)], axis=0)
            cbp = jnp.dot(ccat, bd, preferred_element_type=F32)
            for gi in range(2):
                g = g0 + gi
                cb = cbp[:, L * gi:L * (gi + 1)]
                ms, eas, xds = [], [], []
                for pair in range(heads_per_group // 2):
                    slab = 2 * g + pair
                    acols = []
                    for r in range(2):
                        h = heads_per_group * g + 2 * pair + r
                        acol = jnp.broadcast_to(a_cs[:, h:h + 1], (L, LANES))
                        seg = acol - a_cst[h:h + 1, :]
                        lm = jnp.exp(jnp.where(causal, seg, -jnp.inf))
                        ms.append((cb * lm * dtt[h:h + 1, :]).astype(BF16))
                        acols.append(acol)
                    sel = jnp.where(low_head, acols[0], acols[1])
                    eas.append(jnp.exp(sel))
                    dec = jnp.exp(sel[L - 1:L, :] - sel)
                    xds.append((xs_ref[slab, rows, :] * dtb[rows, LANES * slab:LANES * (slab + 1)] * dec)
                               .astype(BF16))
                ea_g = jnp.concatenate(eas, axis=1)
                cc_ = c_ref[g, rows, :]
                prev = state_ref[g]
                yoff = jnp.dot(cc_, prev.astype(BF16), preferred_element_type=F32) * ea_g
                st_new = jnp.dot(bt_ref[g, c], jnp.concatenate(xds, axis=1), preferred_element_type=F32)
                state_ref[g] = prev * ea_g[L - 1:L, :] + st_new
                xg = jnp.concatenate([xsb_ref[2 * g, rows, :], xsb_ref[2 * g + 1, rows, :]], axis=1)
                zero_x = jnp.zeros_like(xg)
                rhs = jnp.concatenate([jnp.where(head_of_lane == r, xg, zero_x)
                                       for r in range(heads_per_group)], axis=0)
                yd = jnp.dot(jnp.concatenate(ms, axis=1), rhs, preferred_element_type=F32) + yoff
                for pair in range(heads_per_group // 2):
                    slab = 2 * g + pair
                    y_ref[slab, rows, :] = (yd[:, LANES * pair:LANES * (pair + 1)]
                                            + dvec_ref[slab:slab + 1, :] * xs_ref[slab, rows, :])
            yield


def _ssd_stage_b2(x_ref, hb_ref, wz_ref, gnw_ref, wout_ref, o_ref, y_ref, *, tm):
    d_inner, n_xs = SSD_D_INNER, SSD_N_XS
    ssq = jnp.zeros((tm, 1), F32)
    zc = 4 * LANES
    for cc in range(d_inner // zc):
        z = jnp.dot(hb_ref[...], wz_ref[:, zc * cc:zc * (cc + 1)], preferred_element_type=F32)
        for jj in range(zc // LANES):
            j = (zc // LANES) * cc + jj
            yg = y_ref[j] * _silu(z[:, LANES * jj:LANES * (jj + 1)])
            y_ref[j] = yg
            ssq = ssq + jnp.sum(yg * yg, axis=-1, keepdims=True)
        yield
    rinv = lax.rsqrt(ssq * (1.0 / d_inner) + SSM_NORM_EPS)
    yn = jnp.concatenate(
        [(y_ref[j] * rinv * gnw_ref[:, LANES * j:LANES * (j + 1)]).astype(BF16) for j in range(n_xs)],
        axis=1)
    o_ref[...] = x_ref[...] + jnp.dot(yn, wout_ref[...], preferred_element_type=F32)


def _ssd_body(xa_ref, xb_ref, nw_ref, win_hbm, cw_ref, cb_ref, dtb_ref, alog_ref,
              dvec_ref, gnw_ref, wout_hbm, tril_ref, e3_ref, o_ref,
              s_ref, halo_ref, y_ref, state_ref, wz_ref, wxbc_ref, wdt_ref, wout_ref,
              w_stage, sem, hb_st, dt_st, a_st, hb_p, dt_p, a_p, xs_ref, xsb_ref, bt_ref, c_ref,
              *, layer, tm, tiles_per_seq):
    i = pl.program_id(0)

    @pl.when(i == 0)
    def _():
        for r in (hb_st, dt_st, a_st, xs_ref, xsb_ref, bt_ref, c_ref):
            r[...] = jnp.zeros(r.shape, r.dtype)
        d_inner = wz_ref.shape[1]
        conv_dim = wxbc_ref.shape[1]
        rows = w_stage.shape[1]
        _stream_cast(win_hbm.at[layer, pl.ds(0, d_inner), :], wz_ref, w_stage, sem, rows, transpose=True)
        _stream_cast(win_hbm.at[layer, pl.ds(d_inner, conv_dim), :], wxbc_ref, w_stage, sem, rows,
                     transpose=True)
        _stream_cast(wout_hbm.at[layer], wout_ref, w_stage, sem, rows)
        dt_copy = pltpu.make_async_copy(win_hbm.at[layer, pl.ds(d_inner + conv_dim, SSM_HEADS), :],
                                        w_stage.at[0, pl.ds(0, SSM_HEADS), :], sem.at[0])
        dt_copy.start()
        dt_copy.wait()
        wd = w_stage[0, 0:SSM_HEADS, :]
        copies = LANES // SSM_HEADS - 1
        wdt_ref[...] = jnp.concatenate([wd] * copies + [jnp.zeros_like(wd)], axis=0).T.astype(BF16)

    @pl.when(lax.rem(i, tiles_per_seq) == 0)
    def _():
        halo_ref[...] = jnp.zeros(halo_ref.shape, halo_ref.dtype)

    @pl.when((i == 0) | (lax.rem(i - 1, tiles_per_seq) == 0))
    def _():
        state_ref[...] = jnp.zeros(state_ref.shape, state_ref.dtype)

    hb_p[...] = hb_st[...]
    dt_p[...] = dt_st[...]
    a_p[...] = a_st[...]
    _round_robin([
        _ssd_stage_a1(xa_ref, nw_ref, wxbc_ref, wdt_ref, dtb_ref, alog_ref, s_ref, halo_ref,
                      hb_st, dt_st, a_st, tm=tm),
        _ssd_stage_b1(dvec_ref, tril_ref, e3_ref, y_ref, state_ref,
                      xs_ref, xsb_ref, bt_ref, c_ref, dt_p, a_p, tm=tm),
    ])
    _round_robin([
        _ssd_stage_a2(cw_ref, cb_ref, s_ref, xs_ref, xsb_ref, bt_ref, c_ref, tm=tm),
        _ssd_stage_b2(xb_ref, hb_p, wz_ref, gnw_ref, wout_ref, o_ref, y_ref, tm=tm),
    ])


def _ssd_layer(xt, nw, w_in_all, conv_w, conv_b, dt_bias, a_log, d_skip, norm_w, w_out_all, *, layer, seq, tm):
    t, d = xt.shape
    d_inner = SSM_HEADS * SSM_HEAD_DIM
    bc_dim = SSM_GROUPS * SSM_STATE
    conv_dim = d_inner + 2 * bc_dim
    n_xs = d_inner // LANES
    n_b = bc_dim // LANES
    nchunk = tm // SSM_CHUNK
    L = SSM_CHUNK
    copies = 3
    pad = LANES - copies * SSM_HEADS

    def lanes3(v):
        return jnp.concatenate([v] * copies + [jnp.zeros(v.shape[:-1] + (pad,), v.dtype)], axis=-1)

    w_in_t = jnp.swapaxes(w_in_all, 1, 2)
    dtb = lanes3(dt_bias.reshape(1, SSM_HEADS))
    alog = lanes3(a_log.reshape(1, SSM_HEADS))
    dvec = jnp.repeat(d_skip, SSM_HEAD_DIM).reshape(n_xs, LANES)
    tril = jnp.asarray(np.tril(np.ones((L, L), np.float32)), BF16)
    rows_h = np.arange(LANES) % SSM_HEADS
    valid = (np.arange(LANES) < copies * SSM_HEADS)[:, None]
    e3 = jnp.asarray(((rows_h[:, None] == (np.arange(d_inner) // SSM_HEAD_DIM)[None, :]) & valid)
                     .astype(np.float32), BF16)
    n_tiles = t // tm
    body = functools.partial(_ssd_body, layer=layer, tm=tm, tiles_per_seq=seq // tm)
    stage_slots, stage_rows = WEIGHT_STAGE_SLOTS, WEIGHT_STAGE_ELEMS // d
    return pl.pallas_call(
        body,
        out_shape=jax.ShapeDtypeStruct((t, d), F32),
        grid=(n_tiles + 1,),
        in_specs=[
            pl.BlockSpec((tm, d), lambda i: (jnp.minimum(i, n_tiles - 1), 0)),
            pl.BlockSpec((tm, d), lambda i: (jnp.maximum(i - 1, 0), 0)),
            _const_spec((1, d)),
            pl.BlockSpec(memory_space=pl.ANY),
            _const_spec((SSM_CONV, conv_dim)),
            _const_spec((1, conv_dim)),
            _const_spec((1, LANES)),
            _const_spec((1, LANES)),
            _const_spec((n_xs, LANES)),
            _const_spec((1, d_inner)),
            pl.BlockSpec(memory_space=pl.ANY),
            _const_spec((L, L)),
            _const_spec((LANES, d_inner)),
        ],
        out_specs=pl.BlockSpec((tm, d), lambda i: (jnp.maximum(i - 1, 0), 0)),
        scratch_shapes=[
            pltpu.VMEM((n_xs + 2 * n_b, SUBLANES + tm, LANES), F32),
            pltpu.VMEM((n_xs + 2 * n_b, SUBLANES, LANES), F32),
            pltpu.VMEM((n_xs, tm, LANES), F32),
            pltpu.VMEM((SSM_GROUPS, SSM_STATE, 2 * LANES), F32),
            pltpu.VMEM((d, d_inner), BF16),
            pltpu.VMEM((d, conv_dim), BF16),
            pltpu.VMEM((d, LANES), BF16),
            pltpu.VMEM((d_inner, d), BF16),
            pltpu.VMEM((stage_slots, stage_rows, d), F32),
            pltpu.SemaphoreType.DMA((stage_slots,)),
            pltpu.VMEM((tm, d), BF16),
            pltpu.VMEM((tm, LANES), F32),
            pltpu.VMEM((tm, LANES), F32),
            pltpu.VMEM((tm, d), BF16),
            pltpu.VMEM((tm, LANES), F32),
            pltpu.VMEM((tm, LANES), F32),
            pltpu.VMEM((n_xs, tm, LANES), F32),
            pltpu.VMEM((n_xs, tm, LANES), BF16),
            pltpu.VMEM((n_b, nchunk, SSM_STATE, L), BF16),
            pltpu.VMEM((n_b, tm, LANES), BF16),
        ],
        compiler_params=pltpu.CompilerParams(dimension_semantics=("arbitrary",),
                                             vmem_limit_bytes=VMEM_LIMIT_BYTES),
        name="ssd_mixer",
    )(xt, xt, nw.reshape(1, d), w_in_t, conv_w, conv_b.reshape(1, conv_dim), dtb, alog, dvec,
      norm_w.reshape(1, d_inner), w_out_all, tril, e3)


def _ffn_body(x_ref, nw_ref, wup_hbm, cw_ref, cb_ref, wdn_hbm, fnw_ref, o_ref,
              hb_ref, s_ref, halo_ref, act_ref, wu_ref, wg_ref, wdn_ref,
              up_stage, dn_stage, sem, *, layer, tm, fc, tiles_per_seq, final_norm):
    i = pl.program_id(0)
    d_ff = act_ref.shape[1]
    half_slabs = fc // LANES

    @pl.when(i == 0)
    def _():
        _stream_cast(wup_hbm.at[layer, :, pl.ds(0, d_ff)], wu_ref, up_stage, sem, up_stage.shape[1])
        _stream_cast(wup_hbm.at[layer, :, pl.ds(d_ff, d_ff)], wg_ref, up_stage, sem, up_stage.shape[1])
        _stream_cast(wdn_hbm.at[layer], wdn_ref, dn_stage, sem, dn_stage.shape[1])

    @pl.when(lax.rem(i, tiles_per_seq) == 0)
    def _():
        halo_ref[...] = jnp.zeros(halo_ref.shape, halo_ref.dtype)

    x = x_ref[...]
    hb_ref[...] = _rms(x, nw_ref[...], NORM_EPS).astype(BF16)

    for c in range(d_ff // fc):
        slot = c % 2
        ys = []
        for half, w_ref in enumerate((wu_ref, wg_ref)):
            hid = jnp.dot(hb_ref[...], w_ref[:, fc * c:fc * (c + 1)], preferred_element_type=F32)
            for jj in range(half_slabs):
                j = half * half_slabs + jj
                col = half * d_ff + fc * c + LANES * jj
                gs = col // LANES
                x0 = hid[:, LANES * jj:LANES * (jj + 1)]
                s_ref[slot, j, 0:SUBLANES, :] = halo_ref[gs]
                s_ref[slot, j, SUBLANES:SUBLANES + tm, :] = x0
                halo_ref[gs] = x0[tm - SUBLANES:tm]
                y = x0 * cw_ref[FFN_CONV - 1:FFN_CONV, col:col + LANES] + cb_ref[:, col:col + LANES]
                for k in range(1, FFN_CONV):
                    y = y + (s_ref[slot, j, SUBLANES - k:SUBLANES - k + tm, :]
                             * cw_ref[FFN_CONV - 1 - k:FFN_CONV - k, col:col + LANES])
                ys.append(y)
        for jj in range(half_slabs):
            u = ys[jj]
            g = ys[half_slabs + jj]
            act_ref[:, fc * c + LANES * jj:fc * c + LANES * (jj + 1)] = (_silu(g) * u).astype(BF16)

    out = x + jnp.dot(act_ref[...], wdn_ref[...], preferred_element_type=F32)
    if final_norm:
        out = _rms(out, fnw_ref[...], NORM_EPS)
    o_ref[...] = out


def _ffn_layer(xt, nw, w_up_all, conv_w, conv_b, w_down_all, final_w, *, layer, seq, tm, fc, final_norm):
    t, d = xt.shape
    d_ff = w_down_all.shape[1]
    n_tiles = t // tm
    stage_slots = WEIGHT_STAGE_SLOTS
    up_rows = LANES
    dn_rows = WEIGHT_STAGE_ELEMS // d
    assert d % up_rows == 0 and d_ff % dn_rows == 0
    body = functools.partial(_ffn_body, layer=layer, tm=tm, fc=fc, tiles_per_seq=seq // tm,
                             final_norm=final_norm)
    return pl.pallas_call(
        body,
        out_shape=jax.ShapeDtypeStruct((t, d), F32),
        grid=(n_tiles,),
        in_specs=[
            pl.BlockSpec((tm, d), lambda i: (i, 0)),
            _const_spec((1, d)),
            pl.BlockSpec(memory_space=pl.ANY),
            _const_spec((FFN_CONV, 2 * d_ff)),
            _const_spec((1, 2 * d_ff)),
            pl.BlockSpec(memory_space=pl.ANY),
            _const_spec((1, d)),
        ],
        out_specs=pl.BlockSpec((tm, d), lambda i: (i, 0)),
        scratch_shapes=[
            pltpu.VMEM((tm, d), BF16),
            pltpu.VMEM((2, 2 * fc // LANES, SUBLANES + tm, LANES), F32),
            pltpu.VMEM((2 * d_ff // LANES, SUBLANES, LANES), F32),
            pltpu.VMEM((tm, d_ff), BF16),
            pltpu.VMEM((d, d_ff), BF16),
            pltpu.VMEM((d, d_ff), BF16),
            pltpu.VMEM((d_ff, d), BF16),
            pltpu.VMEM((stage_slots, up_rows, d_ff), F32),
            pltpu.VMEM((stage_slots, dn_rows, d), F32),
            pltpu.SemaphoreType.DMA((stage_slots,)),
        ],
        compiler_params=pltpu.CompilerParams(dimension_semantics=("arbitrary",),
                                             vmem_limit_bytes=VMEM_LIMIT_BYTES),
        name="conv_ffn_final" if final_norm else "conv_ffn",
    )(xt, nw.reshape(1, d), w_up_all, conv_w, conv_b.reshape(1, 2 * d_ff), w_down_all,
      final_w.reshape(1, d))


def kernel(x, positions, norm_mix, norm_ffn, norm_final, mix_w_in, pool_w, pool_scale, attn_sinks,
           mix_w_out, ssm_w_in, ssm_conv_w, ssm_conv_b, ssm_dt_bias, ssm_A_log, ssm_D, ssm_norm,
           ssm_w_out, ffn_w_up, ffn_conv_w, ffn_conv_b, ffn_w_down):
    b, s, d = x.shape
    depth = norm_mix.shape[0]
    xt = x.reshape(b * s, d)
    pos = positions.reshape(b * s)
    for i in range(depth):
        j = i // 2
        if i % 2 == 0:
            xt = _mix0_layer(xt, pos, norm_mix[i], mix_w_in[j], pool_w[j], pool_scale[j], attn_sinks[j],
                             mix_w_out[j], seq=s, tm=MIX_TILE)
        else:
            xt = _ssd_layer(xt, norm_mix[i], ssm_w_in, ssm_conv_w[j], ssm_conv_b[j], ssm_dt_bias[j],
                            ssm_A_log[j], ssm_D[j], ssm_norm[j], ssm_w_out, layer=j, seq=s, tm=SSD_TILE)
        xt = _ffn_layer(xt, norm_ffn[i], ffn_w_up, ffn_conv_w[i], ffn_conv_b[i], ffn_w_down,
                        norm_final, layer=i, seq=s, tm=FFN_TILE, fc=FFN_COL_CHUNK,
                        final_norm=(i == depth - 1))
    return xt.reshape(b, s, d)
```

```python
import functools
import math

import numpy as np
import jax
import jax.numpy as jnp
from jax import lax
from jax.experimental import pallas as pl
from jax.experimental.pallas import tpu as pltpu

HEAD_DIM = 64
N_HEADS = 8
N_KV_HEADS = 2
GQ = N_HEADS // N_KV_HEADS
ATT_BLOCK = 128
ROPE_THETA = 10000.0
POOL_WINDOWS = (2, 4, 8, 16)
SSM_HEAD_DIM = 64
SSM_HEADS = 32
SSM_GROUPS = 8
SSM_STATE = 128
SSM_CONV = 4
SSM_CHUNK = 128
FFN_CONV = 3
NORM_EPS = 1e-6
SSM_NORM_EPS = 1e-5

LANES = 128
SUBLANES = 8
VMEM_LIMIT_BYTES = 56 * 1024 * 1024
WEIGHT_STAGE_SLOTS = 4
WEIGHT_STAGE_ELEMS = 256 * 1024

MIX_TILE = 512
SSD_TILE = 256
FFN_TILE = 512
FFN_COL_CHUNK = 256

F32 = jnp.float32
BF16 = jnp.bfloat16


def _rms(x, w, eps):
    ms = jnp.mean(x * x, axis=-1, keepdims=True)
    return x * lax.rsqrt(ms + eps) * w


def _silu(x):
    return x * jax.nn.sigmoid(x)


def _silu_tanh(x):
    h = 0.5 * x
    return h + h * jnp.tanh(h)


def _const_spec(shape):
    nd = len(shape)
    return pl.BlockSpec(shape, lambda i: (0,) * nd, pipeline_mode=pl.Buffered(1))


def _stream_cast(src, dst_ref, stage_ref, sem, chunk_rows, transpose=False):
    rows = dst_ref.shape[1] if transpose else dst_ref.shape[0]
    n = rows // chunk_rows
    slots = stage_ref.shape[0]

    def copy(k):
        return pltpu.make_async_copy(src.at[pl.ds(k * chunk_rows, chunk_rows), :],
                                     stage_ref.at[k % slots], sem.at[k % slots])

    for k in range(min(slots - 1, n)):
        copy(k).start(priority=k % 2)
    for k in range(n):
        if k + slots - 1 < n:
            copy(k + slots - 1).start(priority=(k + slots - 1) % 2)
        copy(k).wait()
        if transpose:
            dst_ref[:, chunk_rows * k:chunk_rows * (k + 1)] = stage_ref[k % slots].T.astype(BF16)
        else:
            dst_ref[chunk_rows * k:chunk_rows * (k + 1), :] = stage_ref[k % slots].astype(BF16)


def _split3_f32(x):
    hi = x.astype(BF16).astype(F32)
    r1 = x - hi
    mid = r1.astype(BF16).astype(F32)
    lo = (r1 - mid).astype(BF16).astype(F32)
    return hi, mid, lo


MIX_POOL_DIM = LANES * len(POOL_WINDOWS)
MIX_Q_DIM = N_HEADS * HEAD_DIM
MIX_NSLAB = MIX_Q_DIM // LANES


def _mix0_stage_a1(x_ref, nw_ref, win_ref, proj_ref, *, tm):
    hb = _rms(x_ref[...], nw_ref[...], NORM_EPS).astype(BF16)
    width = 2 * LANES
    for c in range(proj_ref.shape[1] // width):
        proj_ref[:, width * c:width * (c + 1)] = jnp.dot(hb, win_ref[:, width * c:width * (c + 1)],
                                                         preferred_element_type=F32)
        yield


def _mix0_stage_a2(pos_ref, invf_ref, sgn_ref, poolw_ref, pscale_ref, proj_ref,
                   ubuf, qbuf, kt_ref, vt_ref, pooled_ref, tile_in_seq, *, tm):
    nblk = tm // ATT_BLOCK
    halo = max(POOL_WINDOWS)
    pool_dim, q_dim, nslab = MIX_POOL_DIM, MIX_Q_DIM, MIX_NSLAB

    nfreq = HEAD_DIM // 2
    ngrp = LANES // nfreq
    ang = pos_ref[...].astype(F32) * invf_ref[...]
    grp = lax.broadcasted_iota(jnp.int32, (tm // ngrp, LANES), 1) // nfreq

    def spread(dense):
        rolled = [dense] + [pltpu.roll(dense, nfreq * s, 1) for s in range(1, ngrp)]
        parts = []
        for k in range(ngrp):
            out = rolled[(-k) % ngrp]
            for g in range(1, ngrp):
                out = jnp.where(grp == g, rolled[(g - k) % ngrp], out)
            parts.append(out)
        return jnp.concatenate(parts, axis=0)

    cs = spread(jnp.cos(ang))
    sn = spread(jnp.sin(ang)) * sgn_ref[...]
    lane = lax.broadcasted_iota(jnp.int32, (tm, LANES), 1)
    first_half = (lane & (HEAD_DIM // 2)) == 0
    low_head = lane < HEAD_DIM

    def rope(t):
        up = pltpu.roll(t, LANES - HEAD_DIM // 2, 1)
        dn = pltpu.roll(t, HEAD_DIM // 2, 1)
        return t * cs + jnp.where(first_half, up, dn) * sn

    scale = HEAD_DIM ** -0.5
    for j in range(nslab):
        q = rope(proj_ref[:, pool_dim + LANES * j: pool_dim + LANES * (j + 1)]) * scale
        qa = jnp.where(low_head, q, 0.0).astype(BF16)
        qb = jnp.where(low_head, 0.0, q).astype(BF16)
        for b in range(nblk):
            rows = slice(ATT_BLOCK * b, ATT_BLOCK * (b + 1))
            qbuf[j, b, 0:ATT_BLOCK, :] = qa[rows]
            qbuf[j, b, ATT_BLOCK:2 * ATT_BLOCK, :] = qb[rows]
        yield
    kcol = pool_dim + q_dim
    kr = rope(proj_ref[:, kcol:kcol + LANES])
    for b in range(nblk):
        kt_ref[:, ATT_BLOCK * b:ATT_BLOCK * (b + 1)] = kr[ATT_BLOCK * b:ATT_BLOCK * (b + 1)].T.astype(BF16)
    vt_ref[...] = proj_ref[:, kcol + LANES:kcol + 2 * LANES].astype(BF16)
    yield

    tpos = tile_in_seq * tm + lax.broadcasted_iota(jnp.int32, (tm, 1), 0)
    for g, w in enumerate(POOL_WINDOWS):
        u_g = proj_ref[:, LANES * g:LANES * (g + 1)]
        ubuf[g, halo:halo + tm, :] = u_g
        acc = u_g
        for k in range(1, w):
            acc = acc + ubuf[g, halo - k:halo - k + tm, :]
        cnt = jnp.minimum(tpos + 1, w).astype(F32)
        pooled = acc / cnt - u_g
        pm = jnp.dot(pooled.astype(BF16), poolw_ref[g], preferred_element_type=F32)
        pooled_ref[:, LANES * g:LANES * (g + 1)] = (pm * pscale_ref[:, LANES * g:LANES * (g + 1)]).astype(BF16)
        ubuf[g, 0:halo, :] = ubuf[g, tm:tm + halo, :]
        yield


def _mix0_stage_b1(sink_ref, kcarry, vcarry, qbuf, kt_ref, vt_ref, mixbuf, tile_in_seq, *, tm):
    nblk = tm // ATT_BLOCK
    pool_dim, nslab = MIX_POOL_DIM, MIX_NSLAB

    qi = lax.broadcasted_iota(jnp.int32, (2 * ATT_BLOCK, 2 * ATT_BLOCK), 0) & (ATT_BLOCK - 1)
    kj = lax.broadcasted_iota(jnp.int32, (2 * ATT_BLOCK, 2 * ATT_BLOCK), 1)
    rel = qi + ATT_BLOCK - kj
    band = (rel >= 0) & (rel < ATT_BLOCK)
    first_lim = jnp.where(tile_in_seq == 0, ATT_BLOCK, 0)
    row2 = lax.broadcasted_iota(jnp.int32, (2 * ATT_BLOCK, 1), 0)
    lane_o = lax.broadcasted_iota(jnp.int32, (ATT_BLOCK, LANES), 1)
    for b in range(nblk):
        if b == 0:
            kk = jnp.concatenate([kcarry[...], kt_ref[:, 0:ATT_BLOCK]], axis=1)
            vv = jnp.concatenate([vcarry[...], vt_ref[0:ATT_BLOCK, :]], axis=0)
        else:
            kk = kt_ref[:, ATT_BLOCK * (b - 1):ATT_BLOCK * (b + 1)]
            vv = vt_ref[ATT_BLOCK * (b - 1):ATT_BLOCK * (b + 1), :]
        mask = (band & (kj >= first_lim)) if b == 0 else band
        for j in range(nslab):
            s = jnp.dot(qbuf[j, b], kk, preferred_element_type=F32)
            s = jnp.where(mask, s, -jnp.inf)
            sink = jnp.where(row2 < ATT_BLOCK, sink_ref[j], sink_ref[j + GQ])
            m = jnp.maximum(jnp.max(s, axis=-1, keepdims=True), sink)
            e = jnp.exp(s - m)
            den = jnp.sum(e, axis=-1, keepdims=True) + jnp.exp(sink - m)
            pv = jnp.dot(e.astype(BF16), vv, preferred_element_type=F32) / den
            o = jnp.where(lane_o < HEAD_DIM, pv[0:ATT_BLOCK], pv[ATT_BLOCK:2 * ATT_BLOCK])
            mixbuf[ATT_BLOCK * b:ATT_BLOCK * (b + 1),
                   pool_dim + LANES * j:pool_dim + LANES * (j + 1)] = o.astype(BF16)
            yield
    kcarry[...] = kt_ref[:, tm - ATT_BLOCK:tm]
    vcarry[...] = vt_ref[tm - ATT_BLOCK:tm, :]


def _mix0_stage_b2(x_ref, wout_ref, o_ref, mixbuf):
    o_ref[...] = x_ref[...] + jnp.dot(mixbuf[...], wout_ref[...], preferred_element_type=F32)
    yield


def _round_robin(stages):
    stages = list(stages)
    while stages:
        for st in list(stages):
            if next(st, StopIteration) is StopIteration:
                stages.remove(st)


def _mix0_body(xa_ref, xb_ref, pos_ref, nw_ref, win_ref, invf_ref, sgn_ref, poolw_ref, pscale_ref,
               sink_ref, wout_ref, o_ref, ubuf, kcarry, vcarry, proj_ref, qbuf, kt_ref, vt_ref,
               pooled_ref, mixbuf, *, tm, tiles_per_seq):
    i = pl.program_id(0)
    halo = max(POOL_WINDOWS)
    pool_dim = MIX_POOL_DIM
    tile_a = lax.rem(i, tiles_per_seq)
    tile_b = lax.rem(i - 1, tiles_per_seq)

    @pl.when(i == 0)
    def _():
        for r in (qbuf, kt_ref, vt_ref, pooled_ref, kcarry, vcarry):
            r[...] = jnp.zeros(r.shape, r.dtype)

    @pl.when(tile_a == 0)
    def _():
        ubuf[:, 0:halo, :] = jnp.zeros((len(POOL_WINDOWS), halo, LANES), F32)

    mixbuf[:, 0:pool_dim] = pooled_ref[...]
    _round_robin([
        _mix0_stage_a1(xa_ref, nw_ref, win_ref, proj_ref, tm=tm),
        _mix0_stage_b1(sink_ref, kcarry, vcarry, qbuf, kt_ref, vt_ref, mixbuf, tile_b, tm=tm),
    ])
    _round_robin([
        _mix0_stage_a2(pos_ref, invf_ref, sgn_ref, poolw_ref, pscale_ref, proj_ref,
                       ubuf, qbuf, kt_ref, vt_ref, pooled_ref, tile_a, tm=tm),
        _mix0_stage_b2(xb_ref, wout_ref, o_ref, mixbuf),
    ])


def _mix0_layer(xt, pos, nw, w_in, pool_w, pool_scale, sinks, w_out, *, seq, tm):
    t, d = xt.shape
    pool_dim = LANES * len(POOL_WINDOWS)
    q_dim = N_HEADS * HEAD_DIM
    nslab = q_dim // LANES
    wq = w_in[:, pool_dim:pool_dim + q_dim].reshape(d, N_KV_HEADS, GQ, HEAD_DIM)
    wq = jnp.transpose(wq, (0, 2, 1, 3)).reshape(d, q_dim)
    w_in_p = jnp.concatenate([w_in[:, :pool_dim], wq, w_in[:, pool_dim + q_dim:]], axis=1).astype(BF16)
    wo = w_out[pool_dim:].reshape(N_KV_HEADS, GQ, HEAD_DIM, d)
    wo = jnp.transpose(wo, (1, 0, 2, 3)).reshape(q_dim, d)
    w_out_p = jnp.concatenate([w_out[:pool_dim], wo], axis=0).astype(BF16)
    nfreq = HEAD_DIM // 2
    ngrp = LANES // nfreq
    inv_freq = ROPE_THETA ** (-jnp.arange(0, HEAD_DIM, 2, dtype=F32) / HEAD_DIM)
    invf = jnp.tile(inv_freq, ngrp)[None, :]
    sgn = jnp.asarray(np.where((np.arange(LANES) % HEAD_DIM) < nfreq, -1.0, 1.0)[None, :], F32)
    pos_d = jnp.transpose(pos.reshape(t // tm, ngrp, tm // ngrp), (0, 2, 1))
    pos_d = jnp.repeat(pos_d, nfreq, axis=2).reshape(t // ngrp, LANES)
    mix_in = w_in_p.shape[1]
    nblk = tm // ATT_BLOCK
    halo = max(POOL_WINDOWS)
    n_tiles = t // tm
    body = functools.partial(_mix0_body, tm=tm, tiles_per_seq=seq // tm)
    return pl.pallas_call(
        body,
        out_shape=jax.ShapeDtypeStruct((t, d), F32),
        grid=(n_tiles + 1,),
        in_specs=[
            pl.BlockSpec((tm, d), lambda i: (jnp.minimum(i, n_tiles - 1), 0)),
            pl.BlockSpec((tm, d), lambda i: (jnp.maximum(i - 1, 0), 0)),
            pl.BlockSpec((tm // ngrp, LANES), lambda i: (jnp.minimum(i, n_tiles - 1), 0)),
            _const_spec((1, d)),
            _const_spec((d, mix_in)),
            _const_spec((1, LANES)),
            _const_spec((1, LANES)),
            _const_spec(pool_w.shape),
            _const_spec((1, pool_dim)),
            pl.BlockSpec(memory_space=pltpu.SMEM),
            _const_spec((pool_dim + q_dim, d)),
        ],
        out_specs=pl.BlockSpec((tm, d), lambda i: (jnp.maximum(i - 1, 0), 0)),
        scratch_shapes=[
            pltpu.VMEM((len(POOL_WINDOWS), halo + tm, LANES), F32),
            pltpu.VMEM((ATT_BLOCK, LANES), BF16),
            pltpu.VMEM((ATT_BLOCK, LANES), BF16),
            pltpu.VMEM((tm, mix_in), F32),
            pltpu.VMEM((nslab, nblk, 2 * ATT_BLOCK, LANES), BF16),
            pltpu.VMEM((LANES, tm), BF16),
            pltpu.VMEM((tm, LANES), BF16),
            pltpu.VMEM((tm, pool_dim), BF16),
            pltpu.VMEM((tm, pool_dim + q_dim), BF16),
        ],
        compiler_params=pltpu.CompilerParams(dimension_semantics=("arbitrary",),
                                             vmem_limit_bytes=VMEM_LIMIT_BYTES),
        name="pool_swa_mixer",
    )(xt, xt, pos_d, nw.reshape(1, d), w_in_p, invf, sgn, pool_w.astype(BF16),
      pool_scale.reshape(1, pool_dim), sinks, w_out_p)


SSD_D_INNER = SSM_HEADS * SSM_HEAD_DIM
SSD_BC_DIM = SSM_GROUPS * SSM_STATE
SSD_N_XS = SSD_D_INNER // LANES
SSD_N_B = SSD_BC_DIM // LANES


def _ssd_stage_a1(x_ref, nw_ref, wxbc_ref, wdt_ref, dtb_ref, alog_ref, s_ref, halo_ref,
                  hb_ref, dt_ref, a_ref, *, tm):
    cw_cols = 4 * LANES
    slabs_per_chunk = cw_cols // LANES
    n_cchunks = (SSD_N_XS + 2 * SSD_N_B) // slabs_per_chunk
    first_bc = SSD_N_XS // slabs_per_chunk

    hb_ref[...] = _rms(x_ref[...], nw_ref[...], NORM_EPS).astype(BF16)
    yield
    for cc in list(range(first_bc, n_cchunks)) + list(range(first_bc)):
        hid = jnp.dot(hb_ref[...], wxbc_ref[:, cw_cols * cc:cw_cols * (cc + 1)],
                      preferred_element_type=F32)
        for jj in range(slabs_per_chunk):
            j = slabs_per_chunk * cc + jj
            x0 = hid[:, LANES * jj:LANES * (jj + 1)]
            s_ref[j, 0:SUBLANES, :] = halo_ref[j]
            s_ref[j, SUBLANES:SUBLANES + tm, :] = x0
            halo_ref[j] = x0[tm - SUBLANES:tm]
        yield
    dt = jax.nn.softplus(jnp.dot(hb_ref[...], wdt_ref[...], preferred_element_type=F32) + dtb_ref[...])
    dt_ref[...] = dt
    a_ref[...] = dt * (-jnp.exp(alog_ref[...]))


def _ssd_stage_a2(cw_ref, cb_ref, s_ref, xs_ref, xsb_ref, bt_ref, c_ref, *, tm):
    n_xs, n_b = SSD_N_XS, SSD_N_B
    nchunk = tm // SSM_CHUNK
    L = SSM_CHUNK
    order = list(range(n_xs, n_xs + 2 * n_b)) + list(range(n_xs))
    for idx, j in enumerate(order):
        cols = slice(LANES * j, LANES * (j + 1))
        y = s_ref[j, SUBLANES:SUBLANES + tm, :] * cw_ref[SSM_CONV - 1:SSM_CONV, cols] + cb_ref[:, cols]
        for k in range(1, SSM_CONV):
            y = y + (s_ref[j, SUBLANES - k:SUBLANES - k + tm, :]
                     * cw_ref[SSM_CONV - 1 - k:SSM_CONV - k, cols])
        y = _silu_tanh(y)
        if j < n_xs:
            xs_ref[j] = y
            xsb_ref[j] = y.astype(BF16)
        elif j < n_xs + n_b:
            for c in range(nchunk):
                bt_ref[j - n_xs, c] = y[L * c:L * (c + 1)].T.astype(BF16)
        else:
            c_ref[j - n_xs - n_b] = y.astype(BF16)
        if idx % 4 == 3:
            yield


def _ssd_stage_b1(dvec_ref, tril_ref, e3_ref, y_ref, state_ref,
                  xs_ref, xsb_ref, bt_ref, c_ref, dt_ref, a_ref, *, tm):
    d_inner, n_xs = SSD_D_INNER, SSD_N_XS
    nchunk = tm // SSM_CHUNK
    L = SSM_CHUNK
    heads_per_group = SSM_HEADS // SSM_GROUPS
    gw = heads_per_group * SSM_HEAD_DIM

    dt = dt_ref[...]
    a = a_ref[...]
    lane_t = lax.broadcasted_iota(jnp.int32, (tm, LANES), 1)
    hi, mid, lo = _split3_f32(dt)
    dt3 = jnp.where(lane_t < SSM_HEADS, hi, jnp.where(lane_t < 2 * SSM_HEADS, mid, lo)).astype(BF16)
    dtb = jnp.dot(dt3, e3_ref[...], preferred_element_type=F32)
    yield

    tril = tril_ref[...]
    li = lax.broadcasted_iota(jnp.int32, (L, L), 0)
    si = lax.broadcasted_iota(jnp.int32, (L, L), 1)
    causal = si <= li
    low_head = lax.broadcasted_iota(jnp.int32, (L, LANES), 1) < SSM_HEAD_DIM
    head_of_lane = lax.broadcasted_iota(jnp.int32, (L, gw), 1) // SSM_HEAD_DIM
    zero_blk = jnp.zeros((SSM_STATE, L), BF16)

    for c in range(nchunk):
        rows = slice(L * c, L * (c + 1))
        hi, mid, lo = _split3_f32(a[rows])
        a_cs = (jnp.dot(tril, hi.astype(BF16), preferred_element_type=F32)
                + jnp.dot(tril, mid.astype(BF16), preferred_element_type=F32)
                + jnp.dot(tril, lo.astype(BF16), preferred_element_type=F32))
        a_cs = a_cs * math.log2(math.e)
        a_cst = a_cs.T
        dtt = dt[rows].T

        for gp in range(SSM_GROUPS // 2):
            g0 = 2 * gp
            ccat = jnp.concatenate([c_ref[g0, rows, :], c_ref[g0 + 1, rows, :]], axis=1)
            bd = jnp.concatenate(
                [jnp.concatenate([bt_ref[g0, c], zero_blk], axis=1),
                 jnp.concatenate([zero_blk, bt_ref[g0 + 1, c]], axis=1)], axis=0)
            cbp = jnp.dot(ccat, bd, preferred_element_type=F32)
            for gi in range(2):
                g = g0 + gi
                cb = cbp[:, L * gi:L * (gi + 1)]
                ms, eas, xds = [], [], []
                for pair in range(heads_per_group // 2):
                    slab = 2 * g + pair
                    acols = []
                    for r in range(2):
                        h = heads_per_group * g + 2 * pair + r
                        acol = jnp.broadcast_to(a_cs[:, h:h + 1], (L, LANES))
                        seg = acol - a_cst[h:h + 1, :]
                        lm = jnp.exp2(jnp.where(causal, seg, -jnp.inf))
                        ms.append((cb * lm * dtt[h:h + 1, :]).astype(BF16))
                        acols.append(acol)
                    sel = jnp.where(low_head, acols[0], acols[1])
                    eas.append(jnp.exp2(sel))
                    dec = jnp.exp2(sel[L - 1:L, :] - sel)
                    xds.append((xs_ref[slab, rows, :] * dtb[rows, LANES * slab:LANES * (slab + 1)] * dec)
                               .astype(BF16))
                ea_g = jnp.concatenate(eas, axis=1)
                cc_ = c_ref[g, rows, :]
                prev = state_ref[g]
                yoff = jnp.dot(cc_, prev.astype(BF16), preferred_element_type=F32) * ea_g
                st_new = jnp.dot(bt_ref[g, c], jnp.concatenate(xds, axis=1), preferred_element_type=F32)
                state_ref[g] = prev * ea_g[L - 1:L, :] + st_new
                xg = jnp.concatenate([xsb_ref[2 * g, rows, :], xsb_ref[2 * g + 1, rows, :]], axis=1)
                zero_x = jnp.zeros_like(xg)
                rhs = jnp.concatenate([jnp.where(head_of_lane == r, xg, zero_x)
                                       for r in range(heads_per_group)], axis=0)
                yd = jnp.dot(jnp.concatenate(ms, axis=1), rhs, preferred_element_type=F32) + yoff
                for pair in range(heads_per_group // 2):
                    slab = 2 * g + pair
                    y_ref[slab, rows, :] = (yd[:, LANES * pair:LANES * (pair + 1)]
                                            + dvec_ref[slab:slab + 1, :] * xs_ref[slab, rows, :])
            yield


def _ssd_stage_b2(x_ref, hb_ref, wz_ref, gnw_ref, wout_ref, o_ref, y_ref, *, tm):
    d_inner, n_xs = SSD_D_INNER, SSD_N_XS
    ssq = jnp.zeros((tm, 1), F32)
    zc = 4 * LANES
    for cc in range(d_inner // zc):
        z = jnp.dot(hb_ref[...], wz_ref[:, zc * cc:zc * (cc + 1)], preferred_element_type=F32)
        for jj in range(zc // LANES):
            j = (zc // LANES) * cc + jj
            yg = y_ref[j] * _silu_tanh(z[:, LANES * jj:LANES * (jj + 1)])
            y_ref[j] = yg
            ssq = ssq + jnp.sum(yg * yg, axis=-1, keepdims=True)
        yield
    rinv = lax.rsqrt(ssq * (1.0 / d_inner) + SSM_NORM_EPS)
    yn = jnp.concatenate(
        [(y_ref[j] * rinv * gnw_ref[:, LANES * j:LANES * (j + 1)]).astype(BF16) for j in range(n_xs)],
        axis=1)
    o_ref[...] = x_ref[...] + jnp.dot(yn, wout_ref[...], preferred_element_type=F32)


def _ssd_body(xa_ref, xb_ref, nw_ref, win_hbm, cw_ref, cb_ref, dtb_ref, alog_ref,
              dvec_ref, gnw_ref, wout_hbm, tril_ref, e3_ref, o_ref,
              s_ref, halo_ref, y_ref, state_ref, wz_ref, wxbc_ref, wdt_ref, wout_ref,
              w_stage, sem, hb_st, dt_st, a_st, hb_p, dt_p, a_p, xs_ref, xsb_ref, bt_ref, c_ref,
              *, layer, tm, tiles_per_seq):
    i = pl.program_id(0)

    @pl.when(i == 0)
    def _():
        for r in (hb_st, dt_st, a_st, xs_ref, xsb_ref, bt_ref, c_ref):
            r[...] = jnp.zeros(r.shape, r.dtype)
        d_inner = wz_ref.shape[1]
        conv_dim = wxbc_ref.shape[1]
        rows = w_stage.shape[1]
        _stream_cast(win_hbm.at[layer, pl.ds(0, d_inner), :], wz_ref, w_stage, sem, rows, transpose=True)
        _stream_cast(win_hbm.at[layer, pl.ds(d_inner, conv_dim), :], wxbc_ref, w_stage, sem, rows,
                     transpose=True)
        _stream_cast(wout_hbm.at[layer], wout_ref, w_stage, sem, rows)
        dt_copy = pltpu.make_async_copy(win_hbm.at[layer, pl.ds(d_inner + conv_dim, SSM_HEADS), :],
                                        w_stage.at[0, pl.ds(0, SSM_HEADS), :], sem.at[0])
        dt_copy.start()
        dt_copy.wait()
        wd = w_stage[0, 0:SSM_HEADS, :]
        copies = LANES // SSM_HEADS - 1
        wdt_ref[...] = jnp.concatenate([wd] * copies + [jnp.zeros_like(wd)], axis=0).T.astype(BF16)

    @pl.when(lax.rem(i, tiles_per_seq) == 0)
    def _():
        halo_ref[...] = jnp.zeros(halo_ref.shape, halo_ref.dtype)

    @pl.when((i == 0) | (lax.rem(i - 1, tiles_per_seq) == 0))
    def _():
        state_ref[...] = jnp.zeros(state_ref.shape, state_ref.dtype)

    hb_p[...] = hb_st[...]
    dt_p[...] = dt_st[...]
    a_p[...] = a_st[...]
    _round_robin([
        _ssd_stage_a1(xa_ref, nw_ref, wxbc_ref, wdt_ref, dtb_ref, alog_ref, s_ref, halo_ref,
                      hb_st, dt_st, a_st, tm=tm),
        _ssd_stage_b1(dvec_ref, tril_ref, e3_ref, y_ref, state_ref,
                      xs_ref, xsb_ref, bt_ref, c_ref, dt_p, a_p, tm=tm),
    ])
    _round_robin([
        _ssd_stage_a2(cw_ref, cb_ref, s_ref, xs_ref, xsb_ref, bt_ref, c_ref, tm=tm),
        _ssd_stage_b2(xb_ref, hb_p, wz_ref, gnw_ref, wout_ref, o_ref, y_ref, tm=tm),
    ])


def _ssd_layer(xt, nw, w_in_all, conv_w, conv_b, dt_bias, a_log, d_skip, norm_w, w_out_all, *, layer, seq, tm):
    t, d = xt.shape
    d_inner = SSM_HEADS * SSM_HEAD_DIM
    bc_dim = SSM_GROUPS * SSM_STATE
    conv_dim = d_inner + 2 * bc_dim
    n_xs = d_inner // LANES
    n_b = bc_dim // LANES
    nchunk = tm // SSM_CHUNK
    L = SSM_CHUNK
    copies = 3
    pad = LANES - copies * SSM_HEADS

    def lanes3(v):
        return jnp.concatenate([v] * copies + [jnp.zeros(v.shape[:-1] + (pad,), v.dtype)], axis=-1)

    w_in_t = jnp.swapaxes(w_in_all, 1, 2)
    dtb = lanes3(dt_bias.reshape(1, SSM_HEADS))
    alog = lanes3(a_log.reshape(1, SSM_HEADS))
    dvec = jnp.repeat(d_skip, SSM_HEAD_DIM).reshape(n_xs, LANES)
    tril = jnp.asarray(np.tril(np.ones((L, L), np.float32)), BF16)
    rows_h = np.arange(LANES) % SSM_HEADS
    valid = (np.arange(LANES) < copies * SSM_HEADS)[:, None]
    e3 = jnp.asarray(((rows_h[:, None] == (np.arange(d_inner) // SSM_HEAD_DIM)[None, :]) & valid)
                     .astype(np.float32), BF16)
    n_tiles = t // tm
    body = functools.partial(_ssd_body, layer=layer, tm=tm, tiles_per_seq=seq // tm)
    stage_slots, stage_rows = WEIGHT_STAGE_SLOTS, WEIGHT_STAGE_ELEMS // d
    return pl.pallas_call(
        body,
        out_shape=jax.ShapeDtypeStruct((t, d), F32),
        grid=(n_tiles + 1,),
        in_specs=[
            pl.BlockSpec((tm, d), lambda i: (jnp.minimum(i, n_tiles - 1), 0)),
            pl.BlockSpec((tm, d), lambda i: (jnp.maximum(i - 1, 0), 0)),
            _const_spec((1, d)),
            pl.BlockSpec(memory_space=pl.ANY),
            _const_spec((SSM_CONV, conv_dim)),
            _const_spec((1, conv_dim)),
            _const_spec((1, LANES)),
            _const_spec((1, LANES)),
            _const_spec((n_xs, LANES)),
            _const_spec((1, d_inner)),
            pl.BlockSpec(memory_space=pl.ANY),
            _const_spec((L, L)),
            _const_spec((LANES, d_inner)),
        ],
        out_specs=pl.BlockSpec((tm, d), lambda i: (jnp.maximum(i - 1, 0), 0)),
        scratch_shapes=[
            pltpu.VMEM((n_xs + 2 * n_b, SUBLANES + tm, LANES), F32),
            pltpu.VMEM((n_xs + 2 * n_b, SUBLANES, LANES), F32),
            pltpu.VMEM((n_xs, tm, LANES), F32),
            pltpu.VMEM((SSM_GROUPS, SSM_STATE, 2 * LANES), F32),
            pltpu.VMEM((d, d_inner), BF16),
            pltpu.VMEM((d, conv_dim), BF16),
            pltpu.VMEM((d, LANES), BF16),
            pltpu.VMEM((d_inner, d), BF16),
            pltpu.VMEM((stage_slots, stage_rows, d), F32),
            pltpu.SemaphoreType.DMA((stage_slots,)),
            pltpu.VMEM((tm, d), BF16),
            pltpu.VMEM((tm, LANES), F32),
            pltpu.VMEM((tm, LANES), F32),
            pltpu.VMEM((tm, d), BF16),
            pltpu.VMEM((tm, LANES), F32),
            pltpu.VMEM((tm, LANES), F32),
            pltpu.VMEM((n_xs, tm, LANES), F32),
            pltpu.VMEM((n_xs, tm, LANES), BF16),
            pltpu.VMEM((n_b, nchunk, SSM_STATE, L), BF16),
            pltpu.VMEM((n_b, tm, LANES), BF16),
        ],
        compiler_params=pltpu.CompilerParams(dimension_semantics=("arbitrary",),
                                             vmem_limit_bytes=VMEM_LIMIT_BYTES),
        name="ssd_mixer",
    )(xt, xt, nw.reshape(1, d), w_in_t, conv_w, conv_b.reshape(1, conv_dim), dtb, alog, dvec,
      norm_w.reshape(1, d_inner), w_out_all, tril, e3)


def _ffn_body(x_ref, nw_ref, wup_hbm, cw_ref, cb_ref, wdn_hbm, fnw_ref, o_ref,
              hb_ref, s_ref, halo_ref, act_ref, wu_ref, wg_ref, wdn_ref,
              up_stage, dn_stage, sem, *, layer, tm, fc, tiles_per_seq, final_norm):
    i = pl.program_id(0)
    d_ff = act_ref.shape[1]
    half_slabs = fc // LANES

    @pl.when(i == 0)
    def _():
        _stream_cast(wup_hbm.at[layer, :, pl.ds(0, d_ff)], wu_ref, up_stage, sem, up_stage.shape[1])
        _stream_cast(wup_hbm.at[layer, :, pl.ds(d_ff, d_ff)], wg_ref, up_stage, sem, up_stage.shape[1])
        _stream_cast(wdn_hbm.at[layer], wdn_ref, dn_stage, sem, dn_stage.shape[1])

    @pl.when(lax.rem(i, tiles_per_seq) == 0)
    def _():
        halo_ref[...] = jnp.zeros(halo_ref.shape, halo_ref.dtype)

    x = x_ref[...]
    hb_ref[...] = _rms(x, nw_ref[...], NORM_EPS).astype(BF16)

    for c in range(d_ff // fc):
        slot = c % 2
        ys = []
        for half, w_ref in enumerate((wu_ref, wg_ref)):
            hid = jnp.dot(hb_ref[...], w_ref[:, fc * c:fc * (c + 1)], preferred_element_type=F32)
            for jj in range(half_slabs):
                j = half * half_slabs + jj
                col = half * d_ff + fc * c + LANES * jj
                gs = col // LANES
                x0 = hid[:, LANES * jj:LANES * (jj + 1)]
                s_ref[slot, j, 0:SUBLANES, :] = halo_ref[gs]
                s_ref[slot, j, SUBLANES:SUBLANES + tm, :] = x0
                halo_ref[gs] = x0[tm - SUBLANES:tm]
                y = x0 * cw_ref[FFN_CONV - 1:FFN_CONV, col:col + LANES] + cb_ref[:, col:col + LANES]
                for k in range(1, FFN_CONV):
                    y = y + (s_ref[slot, j, SUBLANES - k:SUBLANES - k + tm, :]
                             * cw_ref[FFN_CONV - 1 - k:FFN_CONV - k, col:col + LANES])
                ys.append(y)
        for jj in range(half_slabs):
            u = ys[jj]
            g = ys[half_slabs + jj]
            act_ref[:, fc * c + LANES * jj:fc * c + LANES * (jj + 1)] = (_silu(g) * u).astype(BF16)

    out = x + jnp.dot(act_ref[...], wdn_ref[...], preferred_element_type=F32)
    if final_norm:
        out = _rms(out, fnw_ref[...], NORM_EPS)
    o_ref[...] = out


def _ffn_layer(xt, nw, w_up_all, conv_w, conv_b, w_down_all, final_w, *, layer, seq, tm, fc, final_norm):
    t, d = xt.shape
    d_ff = w_down_all.shape[1]
    n_tiles = t // tm
    stage_slots = WEIGHT_STAGE_SLOTS
    up_rows = LANES
    dn_rows = WEIGHT_STAGE_ELEMS // d
    assert d % up_rows == 0 and d_ff % dn_rows == 0
    body = functools.partial(_ffn_body, layer=layer, tm=tm, fc=fc, tiles_per_seq=seq // tm,
                             final_norm=final_norm)
    return pl.pallas_call(
        body,
        out_shape=jax.ShapeDtypeStruct((t, d), F32),
        grid=(n_tiles,),
        in_specs=[
            pl.BlockSpec((tm, d), lambda i: (i, 0)),
            _const_spec((1, d)),
            pl.BlockSpec(memory_space=pl.ANY),
            _const_spec((FFN_CONV, 2 * d_ff)),
            _const_spec((1, 2 * d_ff)),
            pl.BlockSpec(memory_space=pl.ANY),
            _const_spec((1, d)),
        ],
        out_specs=pl.BlockSpec((tm, d), lambda i: (i, 0)),
        scratch_shapes=[
            pltpu.VMEM((tm, d), BF16),
            pltpu.VMEM((2, 2 * fc // LANES, SUBLANES + tm, LANES), F32),
            pltpu.VMEM((2 * d_ff // LANES, SUBLANES, LANES), F32),
            pltpu.VMEM((tm, d_ff), BF16),
            pltpu.VMEM((d, d_ff), BF16),
            pltpu.VMEM((d, d_ff), BF16),
            pltpu.VMEM((d_ff, d), BF16),
            pltpu.VMEM((stage_slots, up_rows, d_ff), F32),
            pltpu.VMEM((stage_slots, dn_rows, d), F32),
            pltpu.SemaphoreType.DMA((stage_slots,)),
        ],
        compiler_params=pltpu.CompilerParams(dimension_semantics=("arbitrary",),
                                             vmem_limit_bytes=VMEM_LIMIT_BYTES),
        name="conv_ffn_final" if final_norm else "conv_ffn",
    )(xt, nw.reshape(1, d), w_up_all, conv_w, conv_b.reshape(1, 2 * d_ff), w_down_all,
      final_w.reshape(1, d))


def kernel(x, positions, norm_mix, norm_ffn, norm_final, mix_w_in, pool_w, pool_scale, attn_sinks,
           mix_w_out, ssm_w_in, ssm_conv_w, ssm_conv_b, ssm_dt_bias, ssm_A_log, ssm_D, ssm_norm,
           ssm_w_out, ffn_w_up, ffn_conv_w, ffn_conv_b, ffn_w_down):
    b, s, d = x.shape
    depth = norm_mix.shape[0]
    xt = x.reshape(b * s, d)
    pos = positions.reshape(b * s)
    for i in range(depth):
        j = i // 2
        if i % 2 == 0:
            xt = _mix0_layer(xt, pos, norm_mix[i], mix_w_in[j], pool_w[j], pool_scale[j], attn_sinks[j],
                             mix_w_out[j], seq=s, tm=MIX_TILE)
        else:
            xt = _ssd_layer(xt, norm_mix[i], ssm_w_in, ssm_conv_w[j], ssm_conv_b[j], ssm_dt_bias[j],
                            ssm_A_log[j], ssm_D[j], ssm_norm[j], ssm_w_out, layer=j, seq=s, tm=SSD_TILE)
        xt = _ffn_layer(xt, norm_ffn[i], ffn_w_up, ffn_conv_w[i], ffn_conv_b[i], ffn_w_down,
                        norm_final, layer=i, seq=s, tm=FFN_TILE, fc=FFN_COL_CHUNK,
                        final_norm=(i == depth - 1))
    return xt.reshape(b, s, d)
```
